```python
import math
import jax, jax.numpy as jnp
from jax import lax
import numpy as np

D_MODEL = 1024
BATCH = 8
SEQ = 2048
DEPTH = 1
DEC_BATCH = 128
DEC_SEQ = 8
PAST_LEN = 16384
PAGE_SIZE = 128

M_HEADS = 4
M_HEAD_DIM = 128
M_WIDTH = M_HEADS * M_HEAD_DIM
M_CONV = 4
M_CHUNK = 64
R_HEADS = 8
R_HEAD_DIM = 64
R_WIDTH = R_HEADS * R_HEAD_DIM
R_DECAY_LORA = 64
R_A_LORA = 64
R_GATE_LORA = 128
R_COLS = 3 * R_WIDTH + R_DECAY_LORA + R_A_LORA + R_GATE_LORA
D_FF = 2816
F_CONV = 3
PLE_DIM = 256
EPS = 1e-6
GN_EPS = 64e-5

M_QK = 0
M_V = 2 * M_WIDTH
M_O = 3 * M_WIDTH
M_I = 4 * M_WIDTH
M_F = M_I + M_HEADS
M_END = M_F + M_HEADS
R_START = M_END
R_END = R_START + R_COLS
G_START = R_END
N_IN = G_START + 2 * D_MODEL
RC_R = 0
RC_K = R_WIDTH
RC_V = 2 * R_WIDTH
RC_W = 3 * R_WIDTH
RC_A = RC_W + R_DECAY_LORA
RC_G = RC_A + R_A_LORA

kernel_name = "mlstm_rwkv7_gated_hybrid_step"


def rmsnorm(x, g):
    xf = x.astype(jnp.float32)
    y = xf * lax.rsqrt(jnp.mean(xf * xf, -1, keepdims=True) + EPS)
    return (y * g.astype(jnp.float32)).astype(x.dtype)


def head_layernorm(h, eps):
    B, T, H, N = h.shape
    hf = h.astype(jnp.float32)
    mu = jnp.mean(hf, -1, keepdims=True)
    d = hf - mu
    y = d * lax.rsqrt(jnp.mean(d * d, -1, keepdims=True) + eps)
    return y.reshape(B, T, H * N)


def causal_dwconv(buf, u, w, b):
    W = w.shape[0]
    T = u.shape[1]
    full = jnp.concatenate([buf.astype(u.dtype), u], axis=1)
    y = b + w[W - 1] * full[:, W - 1:W - 1 + T]
    for j in range(W - 1):
        y = y + w[j] * full[:, j:j + T]
    return y, full[:, -(W - 1):]


def mlstm_chunked(q, k, v, li, lf, C0, n0, m0):
    B, T, H, dk = q.shape
    L = math.gcd(T, M_CHUNK)
    nc = T // L

    def to_chunks(a):
        a = a.reshape((B, nc, L) + a.shape[2:])
        return jnp.moveaxis(jnp.moveaxis(a, 3, 2), 1, 0)

    causal = jnp.tril(jnp.ones((L, L), bool))

    def step(carry, inp):
        C, n, m = carry
        qb, kb, vb, ib, fb = inp
        bcum = jnp.cumsum(fb, axis=-1)
        Dm = bcum[..., :, None] - bcum[..., None, :] + ib[..., None, :]
        Dm = jnp.where(causal, Dm, -jnp.inf)
        inter = bcum + m[..., None]
        mt = jnp.maximum(inter, jnp.max(Dm, -1))
        P = jnp.exp(Dm - mt[..., None]) * jnp.einsum('bhld,bhsd->bhls', qb, kb)
        sc = jnp.exp(inter - mt)
        num = sc[..., None] * jnp.einsum('bhld,bhde->bhle', qb, C) + jnp.einsum('bhls,bhse->bhle', P, vb)
        den = sc * jnp.einsum('bhld,bhd->bhl', qb, n) + jnp.sum(P, -1)
        h = num / jnp.maximum(jnp.abs(den), jnp.exp(-mt))[..., None]
        btot = bcum[..., -1]
        wlog = btot[..., None] - bcum + ib
        m_new = jnp.maximum(btot + m, jnp.max(wlog, -1))
        s0 = jnp.exp(btot + m - m_new)
        ws = jnp.exp(wlog - m_new[..., None])
        C_new = s0[..., None, None] * C + jnp.einsum('bhl,bhld,bhle->bhde', ws, kb, vb)
        n_new = s0[..., None] * n + jnp.einsum('bhl,bhld->bhd', ws, kb)
        return (C_new, n_new, m_new), h

    xs = (to_chunks(q), to_chunks(k), to_chunks(v), to_chunks(li), to_chunks(lf))
    (C, n, m), hs = lax.scan(step, (C0, n0, m0), xs)
    hs = jnp.moveaxis(jnp.moveaxis(hs, 0, 1), 2, 3).reshape(B, T, H, -1)
    return hs, C, n, m


def rwkv7_scan(r, w, k, v, a, b, S0):
    def step(S, inp):
        rt, wt, kt, vt, at, bt = inp
        sa = jnp.einsum('bhij,bhj->bhi', S, at)
        S = S * wt[:, :, None, :] + sa[..., None] * bt[:, :, None, :] + vt[..., None] * kt[:, :, None, :]
        return S, jnp.einsum('bhij,bhj->bhi', S, rt)
    xs = tuple(jnp.moveaxis(t, 1, 0) for t in (r, w, k, v, a, b))
    S, ys = lax.scan(step, S0, xs)
    return jnp.moveaxis(ys, 0, 1), S


def decoder_layer(x, pe, states, lp):
    mconv, mC, mn, mm, rshift, rS, fconv = states
    B, T, _ = x.shape
    f32 = jnp.float32
    h = rmsnorm(x, lp['norm1_g'])
    z = h @ lp['w_in']

    qk, mconv_new = causal_dwconv(mconv, z[..., M_QK:M_V], lp['m_conv_w'], lp['m_conv_b'])
    qk = jax.nn.silu(qk)
    q = qk[..., :M_WIDTH].reshape(B, T, M_HEADS, M_HEAD_DIM)
    k = qk[..., M_WIDTH:].reshape(B, T, M_HEADS, M_HEAD_DIM) * (M_HEAD_DIM ** -0.5)
    v = z[..., M_V:M_O].reshape(B, T, M_HEADS, M_HEAD_DIM)
    li = (z[..., M_I:M_F] + lp['m_i_bias']).astype(f32)
    lf = jax.nn.log_sigmoid((z[..., M_F:M_END] + lp['m_f_bias']).astype(f32))
    hm, mC_new, mn_new, mm_new = mlstm_chunked(q.astype(f32), k.astype(f32), v.astype(f32), li, lf,
                                               mC.astype(f32), mn.astype(f32), mm.astype(f32))
    hm = (head_layernorm(hm, EPS) * lp['m_norm_g']).astype(x.dtype)
    hm = jax.nn.sigmoid(z[..., M_O:M_I]) * hm
    y_a = hm @ lp['w_branch_m']

    zr = z[..., R_START:R_END]
    prev = jnp.concatenate([rshift[:, None].astype(zr.dtype), zr[:, :-1]], axis=1)
    xm = zr + (prev - zr) * lp['r_mix']
    rshift_new = zr[:, -1]
    r = xm[..., RC_R:RC_K]
    kr = xm[..., RC_K:RC_V]
    vr = xm[..., RC_V:RC_W]
    wd = xm[..., RC_W:RC_A]
    ad = xm[..., RC_A:RC_G]
    gd = xm[..., RC_G:]
    wlog = -jax.nn.softplus(-(lp['r_w0'] + jnp.tanh(wd) @ lp['r_w2'])) - 0.5
    decay = jnp.exp(-jnp.exp(wlog.astype(f32)))
    a = jax.nn.sigmoid(lp['r_a0'] + ad @ lp['r_a2'])
    g = jax.nn.sigmoid(gd) @ lp['r_g2']

    def heads(t):
        return t.reshape(B, T, R_HEADS, R_HEAD_DIM).astype(f32)

    kk = heads(kr * lp['r_kk'])
    kk = kk / jnp.maximum(jnp.sqrt(jnp.sum(kk * kk, -1, keepdims=True)), 1e-12)
    kr = kr * (1.0 + (a - 1.0) * lp['r_ka'])
    rh, kh, vh = heads(r), heads(kr), heads(vr)
    yr, rS_new = rwkv7_scan(rh, heads(decay), kh, vh, -kk, kk * heads(a), rS.astype(f32))
    yr = head_layernorm(yr, GN_EPS) * lp['r_ln_g'] + lp['r_ln_b']
    bonus = jnp.sum(rh * kh * lp['r_rk'], -1, keepdims=True) * vh
    yr = (yr + bonus.reshape(B, T, R_WIDTH)).astype(x.dtype) * g
    y_b = yr @ lp['w_branch_r']

    gates = jax.nn.sigmoid(z[..., G_START:])
    merged = gates[..., :D_MODEL] * y_a + gates[..., D_MODEL:] * y_b
    x = x + merged @ lp['w_out']

    h2 = rmsnorm(x, lp['norm2_g'])
    u = h2 @ lp['f_up']
    c, fconv_new = causal_dwconv(fconv, u[..., :D_FF], lp['f_conv_w'], lp['f_conv_b'])
    x = x + (jax.nn.gelu(c) * u[..., D_FF:]) @ lp['f_down']

    gate = jax.nn.sigmoid(rmsnorm(x, lp['ple_norm_g']) @ lp['ple_gate_w'])
    x = x + gate * (pe.astype(x.dtype) @ lp['ple_proj'])

    new_states = (mconv_new.astype(mconv.dtype), mC_new.astype(mC.dtype), mn_new.astype(mn.dtype),
                  mm_new.astype(mm.dtype), rshift_new.astype(rshift.dtype), rS_new.astype(rS.dtype),
                  fconv_new.astype(fconv.dtype))
    return x, new_states


def setup_inputs(seed: int = 0) -> dict:
    key = jax.random.key(seed)
    ks = iter(jax.random.split(key, 64))
    L = DEPTH

    def nrm(shape, scale):
        return jax.random.normal(next(ks), shape, jnp.float32) * scale

    def gain(shape):
        return 1.0 + nrm(shape, 0.02)

    def unif(shape, lo, hi):
        return jax.random.uniform(next(ks), shape, jnp.float32, lo, hi)

    return {
        'x_prompt': nrm((BATCH, SEQ, D_MODEL), 1.0),
        'x_sample': nrm((DEC_BATCH, DEC_SEQ, D_MODEL), 1.0),
        'state_mlstm_conv': nrm((L, DEC_BATCH, M_CONV - 1, 2 * M_WIDTH), 1.0),
        'state_mlstm_C': nrm((L, DEC_BATCH, M_HEADS, M_HEAD_DIM, M_HEAD_DIM), 0.05),
        'state_mlstm_n': nrm((L, DEC_BATCH, M_HEADS, M_HEAD_DIM), 0.05),
        'state_mlstm_m': 1.0 + nrm((L, DEC_BATCH, M_HEADS), 0.5),
        'state_rwkv_shift': nrm((L, DEC_BATCH, R_COLS), 1.0),
        'state_rwkv_S': nrm((L, DEC_BATCH, R_HEADS, R_HEAD_DIM, R_HEAD_DIM), 0.1),
        'state_ffn_conv': nrm((L, DEC_BATCH, F_CONV - 1, D_FF), 0.5),
        'p_prompt': nrm((L, BATCH, SEQ, PLE_DIM), 1.0),
        'p_sample': nrm((L, DEC_BATCH, DEC_SEQ, PLE_DIM), 1.0),
        'norm1_g': gain((L, D_MODEL)),
        'w_in': nrm((L, D_MODEL, N_IN), D_MODEL ** -0.5),
        'm_conv_w': nrm((L, M_CONV, 2 * M_WIDTH), 0.5),
        'm_conv_b': nrm((L, 2 * M_WIDTH), 0.01),
        'm_i_bias': nrm((L, M_HEADS), 0.1),
        'm_f_bias': jnp.linspace(3.0, 6.0, M_HEADS, dtype=jnp.float32) + nrm((L, M_HEADS), 0.1),
        'm_norm_g': gain((L, M_WIDTH)),
        'w_branch_m': nrm((L, M_WIDTH, D_MODEL), M_WIDTH ** -0.5),
        'r_mix': unif((L, R_COLS), 0.0, 1.0),
        'r_w0': unif((L, R_WIDTH), -2.0, 1.0),
        'r_w2': nrm((L, R_DECAY_LORA, R_WIDTH), 0.1),
        'r_a0': nrm((L, R_WIDTH), 0.1),
        'r_a2': nrm((L, R_A_LORA, R_WIDTH), 0.5 * R_A_LORA ** -0.5),
        'r_g2': nrm((L, R_GATE_LORA, R_WIDTH), R_GATE_LORA ** -0.5),
        'r_kk': 0.85 + nrm((L, R_WIDTH), 0.05),
        'r_ka': 1.0 + nrm((L, R_WIDTH), 0.05),
        'r_rk': nrm((L, R_HEADS, R_HEAD_DIM), 0.1),
        'r_ln_g': gain((L, R_WIDTH)),
        'r_ln_b': nrm((L, R_WIDTH), 0.01),
        'w_branch_r': nrm((L, R_WIDTH, D_MODEL), R_WIDTH ** -0.5),
        'w_out': nrm((L, D_MODEL, D_MODEL), D_MODEL ** -0.5),
        'norm2_g': gain((L, D_MODEL)),
        'f_up': nrm((L, D_MODEL, 2 * D_FF), D_MODEL ** -0.5),
        'f_conv_w': nrm((L, F_CONV, D_FF), 0.5),
        'f_conv_b': nrm((L, D_FF), 0.01),
        'f_down': nrm((L, D_FF, D_MODEL), D_FF ** -0.5),
        'ple_norm_g': gain((L, D_MODEL)),
        'ple_gate_w': nrm((L, D_MODEL, D_MODEL), D_MODEL ** -0.5),
        'ple_proj': nrm((L, PLE_DIM, D_MODEL), 0.5 * PLE_DIM ** -0.5),
        'final_norm_g': gain((D_MODEL,)),
    }


def reference(x_prompt, x_sample, state_mlstm_conv, state_mlstm_C, state_mlstm_n, state_mlstm_m,
              state_rwkv_shift, state_rwkv_S, state_ffn_conv, p_prompt, p_sample,
              norm1_g, w_in, m_conv_w, m_conv_b, m_i_bias, m_f_bias, m_norm_g, w_branch_m,
              r_mix, r_w0, r_w2, r_a0, r_a2, r_g2, r_kk, r_ka, r_rk, r_ln_g, r_ln_b, w_branch_r,
              w_out, norm2_g, f_up, f_conv_w, f_conv_b, f_down, ple_norm_g, ple_gate_w, ple_proj,
              final_norm_g):
    y_p, y_s = x_prompt, x_sample
    Bp = x_prompt.shape[0]
    dt = x_prompt.dtype
    new_p = [[] for _ in range(7)]
    new_s = [[] for _ in range(7)]
    for i in range(DEPTH):
        lp = dict(norm1_g=norm1_g[i], w_in=w_in[i], m_conv_w=m_conv_w[i], m_conv_b=m_conv_b[i],
                  m_i_bias=m_i_bias[i], m_f_bias=m_f_bias[i], m_norm_g=m_norm_g[i],
                  w_branch_m=w_branch_m[i], r_mix=r_mix[i], r_w0=r_w0[i], r_w2=r_w2[i],
                  r_a0=r_a0[i], r_a2=r_a2[i], r_g2=r_g2[i], r_kk=r_kk[i], r_ka=r_ka[i],
                  r_rk=r_rk[i], r_ln_g=r_ln_g[i], r_ln_b=r_ln_b[i], w_branch_r=w_branch_r[i],
                  w_out=w_out[i], norm2_g=norm2_g[i], f_up=f_up[i], f_conv_w=f_conv_w[i],
                  f_conv_b=f_conv_b[i], f_down=f_down[i], ple_norm_g=ple_norm_g[i],
                  ple_gate_w=ple_gate_w[i], ple_proj=ple_proj[i])
        init_p = (jnp.zeros((Bp, M_CONV - 1, 2 * M_WIDTH), dt),
                  jnp.zeros((Bp, M_HEADS, M_HEAD_DIM, M_HEAD_DIM), dt),
                  jnp.zeros((Bp, M_HEADS, M_HEAD_DIM), dt),
                  jnp.zeros((Bp, M_HEADS), dt),
                  jnp.zeros((Bp, R_COLS), dt),
                  jnp.zeros((Bp, R_HEADS, R_HEAD_DIM, R_HEAD_DIM), dt),
                  jnp.zeros((Bp, F_CONV - 1, D_FF), dt))
        y_p, st_p = decoder_layer(y_p, p_prompt[i], init_p, lp)
        past_s = (state_mlstm_conv[i], state_mlstm_C[i], state_mlstm_n[i], state_mlstm_m[i],
                  state_rwkv_shift[i], state_rwkv_S[i], state_ffn_conv[i])
        y_s, st_s = decoder_layer(y_s, p_sample[i], past_s, lp)
        for j in range(7):
            new_p[j].append(st_p[j])
            new_s[j].append(st_s[j])
    y_prompt = rmsnorm(y_p, final_norm_g)
    y_sample = rmsnorm(y_s, final_norm_g)
    p_conv, p_C, p_n, p_m, p_shift, p_S, p_fconv = [jnp.stack(t, 0) for t in new_p]
    s_conv, s_C, s_n, s_m, s_shift, s_S, s_fconv = [jnp.stack(t, 0) for t in new_s]
    return (y_prompt, y_sample, p_conv, p_C, p_n, p_m, p_shift, p_S, p_fconv,
            s_conv, s_C, s_n, s_m, s_shift, s_S, s_fconv)
```

```python
import functools

import jax
import jax.numpy as jnp
from jax import lax
from jax.experimental import pallas as pl
from jax.experimental.pallas import tpu as pltpu

F32 = jnp.float32
BF16 = jnp.bfloat16

M_HEADS = 4
M_HEAD_DIM = 128
M_WIDTH = M_HEADS * M_HEAD_DIM
M_CONV = 4
R_HEADS = 8
R_HEAD_DIM = 64
R_WIDTH = R_HEADS * R_HEAD_DIM
R_DECAY_LORA = 64
R_A_LORA = 64
R_GATE_LORA = 128
R_COLS = 3 * R_WIDTH + R_DECAY_LORA + R_A_LORA + R_GATE_LORA
F_CONV = 3
EPS = 1e-6
GN_EPS = 64e-5

LANES = 128
SUBLANES = 8
VMEM_LIMIT_BYTES = 56 * 1024 * 1024

MLSTM_CHUNK = 128
SCAN_CHUNK = 64
SCAN_BATCH_GROUP = 2


def _cparams(*sem):
    return pltpu.CompilerParams(dimension_semantics=sem, vmem_limit_bytes=VMEM_LIMIT_BYTES)


def _sigmoid(x):
    return 1.0 / (1.0 + jnp.exp(-x))


def _softplus(x):
    return jnp.maximum(x, 0.0) + jnp.log1p(jnp.exp(-jnp.abs(x)))


def _log_sigmoid(x):
    return -_softplus(-x)


def _gelu_tanh(x):
    return 0.5 * x * (1.0 + jnp.tanh(0.7978845608028654 * (x + 0.044715 * (x * x * x))))


def _bdot(a, b):
    return jnp.dot(a.astype(BF16), b.astype(BF16), preferred_element_type=F32)


def _rmsnorm(x, g):
    return x * lax.rsqrt(jnp.mean(x * x, axis=-1, keepdims=True) + EPS) * g


def _split_dot(x, m01, parts):
    acc = None
    r = x
    for p in range(parts):
        h = r.astype(BF16)
        d = jnp.dot(h, m01, preferred_element_type=F32)
        acc = d if acc is None else acc + d
        if p + 1 < parts:
            r = r - h.astype(F32)
    return acc


def _split_dot_lhs01(m01, x, parts):
    acc = None
    r = x
    for p in range(parts):
        h = r.astype(BF16)
        d = jnp.dot(m01, h, preferred_element_type=F32)
        acc = d if acc is None else acc + d
        if p + 1 < parts:
            r = r - h.astype(F32)
    return acc


def _pair_ones():
    r = lax.broadcasted_iota(jnp.int32, (LANES, LANES), 0) // R_HEAD_DIM
    c = lax.broadcasted_iota(jnp.int32, (LANES, LANES), 1) // R_HEAD_DIM
    return jnp.where(r == c, 1.0, 0.0).astype(BF16)


def _head_sum64(x, ones):
    outs = [_split_dot(x[:, s:s + LANES], ones, 3) for s in range(0, x.shape[1], LANES)]
    return outs[0] if len(outs) == 1 else jnp.concatenate(outs, axis=1)


def _shift_rows(x, tail, s):
    n, rt = x.shape[0], tail.shape[0]
    if s % SUBLANES == 0:
        if s == n:
            return tail[rt - s:]
        return jnp.concatenate([tail[rt - s:], x[:n - s]], axis=0)
    assert s < SUBLANES
    r = pltpu.roll(x, s, 0)
    t8 = pltpu.roll(tail[rt - SUBLANES:], s, 0)
    row = lax.broadcasted_iota(jnp.int32, (SUBLANES, x.shape[1]), 0)
    first = jnp.where(row < s, t8, r[:SUBLANES])
    if n == SUBLANES:
        return first
    return jnp.concatenate([first, r[SUBLANES:]], axis=0)


def _norm_proj_kernel(x_ref, g_ref, w_ref, *out_refs, widths):
    h = _rmsnorm(x_ref[...], g_ref[...]).astype(BF16)
    c = 0
    for o_ref, wd in zip(out_refs, widths):
        o_ref[...] = jnp.dot(h, w_ref[:, c:c + wd], preferred_element_type=F32)
        c += wd


def _norm_proj(x, g, w_bf16, widths, tm, name):
    n, d = x.shape
    wtot = w_bf16.shape[1]
    assert sum(widths) == wtot and n % tm == 0
    return pl.pallas_call(
        functools.partial(_norm_proj_kernel, widths=tuple(widths)),
        grid=(n // tm,),
        in_specs=[pl.BlockSpec((tm, d), lambda i: (i, 0)),
                  pl.BlockSpec((1, d), lambda i: (0, 0)),
                  pl.BlockSpec((d, wtot), lambda i: (0, 0))],
        out_specs=[pl.BlockSpec((tm, wd), lambda i: (i, 0)) for wd in widths],
        out_shape=[jax.ShapeDtypeStruct((n, wd), F32) for wd in widths],
        compiler_params=_cparams("parallel"),
        name=name,
    )(x, g, w_bf16)


def _mconv_kernel(x_ref, st_ref, w_ref, b_ref, sc_ref, o_ref, tail_scr, *, nb, rt):
    @pl.when(pl.program_id(1) == 0)
    def _():
        tail_scr[...] = st_ref[0]

    x = x_ref[0]
    tail = tail_scr[...]
    y = b_ref[...] + w_ref[M_CONV - 1:M_CONV, :] * x
    for j in range(M_CONV - 1):
        y = y + w_ref[j:j + 1, :] * _shift_rows(x, tail, (M_CONV - 1 - j) * nb)
    tail_scr[...] = x[x.shape[0] - rt:]
    o_ref[0] = y * _sigmoid(y) * sc_ref[...]


def _mconv(x, state, w8, b, scale, nb, tr, name):
    g, r, c = x.shape
    rt = state.shape[1]
    assert r % tr == 0 and tr >= rt
    return pl.pallas_call(
        functools.partial(_mconv_kernel, nb=nb, rt=rt),
        grid=(g, r // tr),
        in_specs=[pl.BlockSpec((1, tr, c), lambda a, i: (a, i, 0)),
                  pl.BlockSpec((1, rt, c), lambda a, i: (a, 0, 0)),
                  pl.BlockSpec((SUBLANES, c), lambda a, i: (0, 0)),
                  pl.BlockSpec((1, c), lambda a, i: (0, 0)),
                  pl.BlockSpec((1, c), lambda a, i: (0, 0))],
        out_specs=pl.BlockSpec((1, tr, c), lambda a, i: (a, i, 0)),
        out_shape=jax.ShapeDtypeStruct((g, r, c), F32),
        scratch_shapes=[pltpu.VMEM((rt, c), F32)],
        compiler_params=_cparams("parallel", "arbitrary"),
        name=name,
    )(x, state, w8, b, scale)


def _mlstm_kernel(qk_ref, v_ref, og_ref, gt_ref, gb_ref, ng_ref, c0_ref, n0_ref, m0_ref,
                  h_ref, ct_ref, nt_ref, mt_ref, c_scr, n_scr, m_scr, *, lb):
    lp, hd, nh = MLSTM_CHUNK, M_HEAD_DIM, M_HEADS

    @pl.when(pl.program_id(1) == 0)
    def _():
        c_scr[...] = c0_ref[0]
        n_scr[...] = n0_ref[0]
        m_scr[...] = m0_ref[0]

    def pad(x):
        if lb == lp:
            return x
        return jnp.concatenate([x, jnp.zeros((lp - lb, x.shape[1]), x.dtype)], axis=0)

    qk = pad(qk_ref[0])
    v = pad(v_ref[0])
    gates = pad(gt_ref[0]) + gb_ref[...]
    row = lax.broadcasted_iota(jnp.int32, (lp, LANES), 0)
    lane = lax.broadcasted_iota(jnp.int32, (lp, LANES), 1)
    valid = row < lb
    li = jnp.where(valid, gates, -jnp.inf)
    lf = jnp.where(valid & (lane >= nh) & (lane < 2 * nh), _log_sigmoid(gates), 0.0)
    sq_r = lax.broadcasted_iota(jnp.int32, (lp, lp), 0)
    sq_c = lax.broadcasted_iota(jnp.int32, (lp, lp), 1)
    causal = sq_r >= sq_c
    tril = jnp.where(causal, 1.0, 0.0).astype(BF16)
    bcum = _split_dot_lhs01(tril, lf, 3)
    xg = jnp.where(lane < nh, li, bcum)
    xg_t = xg.T

    for h in range(nh):
        bcol = xg[:, nh + h:nh + h + 1]
        icol = xg[:, h:h + 1]
        brow = xg_t[nh + h:nh + h + 1, :]
        irow = xg_t[h:h + 1, :]
        m_prev = m_scr[h:h + 1, :][:, 0:1]
        q = qk[:, h * hd:(h + 1) * hd]
        k = qk[:, M_WIDTH + h * hd:M_WIDTH + (h + 1) * hd]
        vh = v[:, h * hd:(h + 1) * hd]
        qb, kb, vb = q.astype(BF16), k.astype(BF16), vh.astype(BF16)
        c_h = c_scr[h]
        n_h = n_scr[h:h + 1, :]

        dm = jnp.where(causal, bcol - brow + irow, -jnp.inf)
        inter = bcol + m_prev
        mt = jnp.maximum(inter, jnp.max(dm, axis=1, keepdims=True))
        s = lax.dot_general(qb, kb, (((1,), (1,)), ((), ())), preferred_element_type=F32)
        p = jnp.exp(dm - mt) * s
        sc = jnp.exp(inter - mt)
        num = sc * jnp.dot(qb, c_h.astype(BF16), preferred_element_type=F32) \
            + jnp.dot(p.astype(BF16), vb, preferred_element_type=F32)
        den = sc * jnp.sum(q * n_h, axis=1, keepdims=True) + jnp.sum(p, axis=1, keepdims=True)
        hh = num / jnp.maximum(jnp.abs(den), jnp.exp(-mt))
        mu = jnp.mean(hh, axis=1, keepdims=True)
        d = hh - mu
        y = d * lax.rsqrt(jnp.mean(d * d, axis=1, keepdims=True) + EPS) * ng_ref[:, h * hd:(h + 1) * hd]
        h_ref[0, :, h * hd:(h + 1) * hd] = _sigmoid(og_ref[0, :, h * hd:(h + 1) * hd]) * y[:lb]

        btot = bcol[lp - 1:lp, :]
        wlog = btot - bcol + icol
        m_new = jnp.maximum(btot + m_prev, jnp.max(wlog, axis=0, keepdims=True))
        s0 = jnp.exp(btot + m_prev - m_new)
        wk = jnp.exp(wlog - m_new) * k
        c_scr[h] = s0 * c_h + jnp.dot(wk.T.astype(BF16), vb, preferred_element_type=F32)
        n_scr[h:h + 1, :] = s0 * n_h + jnp.sum(wk, axis=0, keepdims=True)
        m_scr[h:h + 1, :] = jnp.broadcast_to(m_new, (1, LANES))

    @pl.when(pl.program_id(1) == pl.num_programs(1) - 1)
    def _():
        ct_ref[0] = c_scr[...]
        nt_ref[0] = n_scr[...]
        mt_ref[0] = m_scr[...]


def _mlstm(qk, v, og, gt, gbias, ng, c0, n0, m0, name):
    b, t, _ = qk.shape
    lb = min(t, MLSTM_CHUNK)
    assert t % lb == 0
    nc = t // lb
    tok = lambda w: pl.BlockSpec((1, lb, w), lambda a, c: (a, c, 0))
    const = lambda shp: pl.BlockSpec(shp, lambda a, c: tuple(0 for _ in shp))
    st4 = pl.BlockSpec((1, M_HEADS, M_HEAD_DIM, M_HEAD_DIM), lambda a, c: (a, 0, 0, 0))
    st3 = pl.BlockSpec((1, SUBLANES, LANES), lambda a, c: (a, 0, 0))
    return pl.pallas_call(
        functools.partial(_mlstm_kernel, lb=lb),
        grid=(b, nc),
        in_specs=[tok(2 * M_WIDTH), tok(M_WIDTH), tok(M_WIDTH), tok(LANES),
                  const((1, LANES)), const((1, M_WIDTH)), st4, st3, st3],
        out_specs=[tok(M_WIDTH), st4, st3, st3],
        out_shape=[jax.ShapeDtypeStruct((b, t, M_WIDTH), F32),
                   jax.ShapeDtypeStruct((b, M_HEADS, M_HEAD_DIM, M_HEAD_DIM), F32),
                   jax.ShapeDtypeStruct((b, SUBLANES, LANES), F32),
                   jax.ShapeDtypeStruct((b, SUBLANES, LANES), F32)],
        scratch_shapes=[pltpu.VMEM((M_HEADS, M_HEAD_DIM, M_HEAD_DIM), F32),
                        pltpu.VMEM((SUBLANES, LANES), F32),
                        pltpu.VMEM((SUBLANES, LANES), F32)],
        compiler_params=_cparams("parallel", "arbitrary"),
        name=name,
    )(qk, v, og, gt, gbias, ng, c0, n0, m0)


def _rprep_kernel(z_ref, st_ref, mix_ref, w2_ref, a2_ref, g2_ref, w0_ref, a0_ref, kk_ref, ka_ref,
                  r_o, w_o, k_o, v_o, a_o, b_o, g_o, tail_scr, *, nb, rt):
    @pl.when(pl.program_id(1) == 0)
    def _():
        tail_scr[...] = st_ref[0]

    z = z_ref[0]
    prev = _shift_rows(z, tail_scr[...], nb)
    tail_scr[...] = z[z.shape[0] - rt:]
    xm = z + (prev - z) * mix_ref[...]
    w = R_WIDTH
    r, kr, vr = xm[:, 0:w], xm[:, w:2 * w], xm[:, 2 * w:3 * w]
    lora_in = xm[:, 3 * w:3 * w + LANES]
    gd = xm[:, 3 * w + LANES:3 * w + 2 * LANES]
    lw = _bdot(jnp.tanh(lora_in), w2_ref[...])
    la = _bdot(lora_in, a2_ref[...])
    wlog = -_softplus(-(w0_ref[...] + lw)) - 0.5
    a = _sigmoid(a0_ref[...] + la)
    ones = _pair_ones()
    kk = kr * kk_ref[...]
    kk = kk / jnp.maximum(jnp.sqrt(_head_sum64(kk * kk, ones)), 1e-12)
    r_o[0] = r
    w_o[0] = jnp.exp(-jnp.exp(wlog))
    k_o[0] = kr * (1.0 + (a - 1.0) * ka_ref[...])
    v_o[0] = vr
    a_o[0] = -kk
    b_o[0] = kk * a
    g_o[0] = _bdot(_sigmoid(gd), g2_ref[...])


def _rprep(z, state, mix, w2p, a2p, g2, w0, a0, kkw, kaw, nb, tr, name):
    g, r, c = z.shape
    rt = state.shape[1]
    assert r % tr == 0 and tr >= rt
    w = R_WIDTH
    const = lambda shp: pl.BlockSpec(shp, lambda a, i: tuple(0 for _ in shp))
    tok = pl.BlockSpec((1, tr, w), lambda a, i: (a, i, 0))
    return pl.pallas_call(
        functools.partial(_rprep_kernel, nb=nb, rt=rt),
        grid=(g, r // tr),
        in_specs=[pl.BlockSpec((1, tr, c), lambda a, i: (a, i, 0)),
                  pl.BlockSpec((1, rt, c), lambda a, i: (a, 0, 0)),
                  const((1, c)), const((LANES, w)), const((LANES, w)), const((LANES, w)),
                  const((1, w)), const((1, w)), const((1, w)), const((1, w))],
        out_specs=[tok] * 7,
        out_shape=[jax.ShapeDtypeStruct((g, r, w), F32)] * 7,
        scratch_shapes=[pltpu.VMEM((rt, c), F32)],
        compiler_params=_cparams("parallel", "arbitrary"),
        name=name,
    )(z, state, mix, w2p, a2p, g2, w0, a0, kkw, kaw)


def _rscan_kernel(r_ref, w_ref, k_ref, v_ref, a_ref, b_ref, s0_ref, y_ref, st_ref, s_scr, *, bg, steps):
    hp = R_HEADS // 2
    npair = bg * hp
    hd = R_HEAD_DIM
    tc = pl.program_id(1)

    @pl.when(tc == 0)
    def _():
        s_scr[...] = s0_ref[...]

    y_ref[...] = jnp.zeros(y_ref.shape, F32)
    ones = _pair_ones()
    lane = lax.broadcasted_iota(jnp.int32, (hd, LANES), 1) & (hd - 1)
    diag = lane == lax.broadcasted_iota(jnp.int32, (hd, LANES), 0)
    pairs = [(bi, p) for bi in range(bg) for p in range(hp)]
    sl = lambda q: slice(q * hd, (q + 1) * hd)

    def step8(t8, carry):
        tiles = [[ref[bi, t8, :, p * LANES:(p + 1) * LANES] for bi, p in pairs]
                 for ref in (r_ref, w_ref, k_ref, v_ref, a_ref, b_ref)]
        s_cur = [s_scr[sl(q), :] for q in range(npair)]
        for j in range(SUBLANES):
            r_t, w_t, k_t, v_t, a_t, b_t = [[tl[q][j:j + 1, :] for q in range(npair)] for tl in tiles]
            prod = jnp.concatenate([s_cur[q] * a_t[q] for q in range(npair)], axis=0)
            sa = _split_dot(prod, ones, 2)
            vdiag = jnp.concatenate([jnp.where(diag, v_t[q], 0.0) for q in range(npair)], axis=0)
            vcol = jnp.dot(vdiag.astype(BF16), ones, preferred_element_type=F32)
            s_cur = [s_cur[q] * w_t[q] + sa[sl(q), :] * b_t[q] + vcol[sl(q), :] * k_t[q] for q in range(npair)]
            outs = jnp.concatenate([s_cur[q] * r_t[q] for q in range(npair)], axis=0)
            yb = jnp.dot(outs.astype(BF16), ones, preferred_element_type=F32)
            hit = lane == t8 * SUBLANES + j
            for q in range(npair):
                y_ref[0, sl(q), :] = jnp.where(hit, yb[sl(q), :], y_ref[0, sl(q), :])
        for q in range(npair):
            s_scr[sl(q), :] = s_cur[q]
        return carry

    lax.fori_loop(0, steps // SUBLANES, step8, 0)

    @pl.when(tc == pl.num_programs(1) - 1)
    def _():
        st_ref[...] = s_scr[...]


def _rscan(ops, s0p, name):
    nbatch, nsteps, _ = ops[0].shape
    bg = SCAN_BATCH_GROUP
    steps = min(nsteps, SCAN_CHUNK)
    assert nbatch % bg == 0 and nsteps % steps == 0 and steps % SUBLANES == 0
    nchunk = nsteps // steps
    rows = bg * (R_HEADS // 2) * R_HEAD_DIM
    ops = [o.reshape(nbatch, nsteps // SUBLANES, SUBLANES, R_WIDTH) for o in ops]
    opspec = pl.BlockSpec((bg, steps // SUBLANES, SUBLANES, R_WIDTH), lambda g, c: (g, c, 0, 0))
    stspec = pl.BlockSpec((rows, LANES), lambda g, c: (g, 0))
    return pl.pallas_call(
        functools.partial(_rscan_kernel, bg=bg, steps=steps),
        grid=(nbatch // bg, nchunk),
        in_specs=[opspec] * 6 + [stspec],
        out_specs=[pl.BlockSpec((1, rows, LANES), lambda g, c: (c, g, 0)), stspec],
        out_shape=[jax.ShapeDtypeStruct((nchunk, nbatch * rows // bg, LANES), F32),
                   jax.ShapeDtypeStruct(s0p.shape, F32)],
        scratch_shapes=[pltpu.VMEM((rows, LANES), F32)],
        compiler_params=_cparams("parallel", "arbitrary"),
        name=name,
    )(*ops, s0p)


def _merge_kernel(x_ref, hm_ref, y_ref, r_ref, k_ref, v_ref, g_ref, zg_ref, rk_ref, lng_ref, lnb_ref,
                  wbm_ref, wbr_ref, wo_ref, o_ref):
    ones = _pair_ones()
    y = y_ref[...]
    inv = 1.0 / R_HEAD_DIM
    mu = _head_sum64(y, ones) * inv
    d = y - mu
    var = _head_sum64(d * d, ones) * inv
    yn = d * lax.rsqrt(var + GN_EPS) * lng_ref[...] + lnb_ref[...]
    v = v_ref[...]
    bonus = _head_sum64(r_ref[...] * k_ref[...] * rk_ref[...], ones) * v
    yr = (yn + bonus) * g_ref[...]
    y_b = _bdot(yr, wbr_ref[...])
    y_a = _bdot(hm_ref[...], wbm_ref[...])
    dm = x_ref.shape[1]
    merged = _sigmoid(zg_ref[:, 0:dm]) * y_a + _sigmoid(zg_ref[:, dm:2 * dm]) * y_b
    o_ref[...] = x_ref[...] + _bdot(merged, wo_ref[...])


def _merge(x, hm, y, r, k, v, g, zg, rk, lng, lnb, wbm, wbr, wo, tm, name):
    n, d = x.shape
    w = R_WIDTH
    tok = lambda c: pl.BlockSpec((tm, c), lambda i: (i, 0))
    const = lambda shp: pl.BlockSpec(shp, lambda i: tuple(0 for _ in shp))
    return pl.pallas_call(
        _merge_kernel,
        grid=(n // tm,),
        in_specs=[tok(d), tok(M_WIDTH), tok(w), tok(w), tok(w), tok(w), tok(w), tok(2 * d),
                  const((1, w)), const((1, w)), const((1, w)),
                  const((M_WIDTH, d)), const((w, d)), const((d, d))],
        out_specs=tok(d),
        out_shape=jax.ShapeDtypeStruct((n, d), F32),
        compiler_params=_cparams("parallel"),
        name=name,
    )(x, hm, y, r, k, v, g, zg, rk, lng, lnb, wbm, wbr, wo)


def _ffn_out_kernel(uc_ref, ug_ref, st_ref, x_ref, pe_ref, cw_ref, cb_ref, fd_ref, png_ref, pgw_ref,
                    ppj_ref, fng_ref, y_ref, tail_scr, *, nb, rt):
    @pl.when(pl.program_id(1) == 0)
    def _():
        tail_scr[...] = st_ref[0]

    uc = uc_ref[0]
    tail = tail_scr[...]
    c = cb_ref[...] + cw_ref[F_CONV - 1:F_CONV, :] * uc
    for j in range(F_CONV - 1):
        c = c + cw_ref[j:j + 1, :] * _shift_rows(uc, tail, (F_CONV - 1 - j) * nb)
    tail_scr[...] = uc[uc.shape[0] - rt:]
    x2 = x_ref[0] + _bdot(_gelu_tanh(c) * ug_ref[0], fd_ref[...])
    gate = _sigmoid(_bdot(_rmsnorm(x2, png_ref[...]), pgw_ref[...]))
    x3 = x2 + gate * _bdot(pe_ref[0], ppj_ref[...])
    y_ref[0] = _rmsnorm(x3, fng_ref[...])


def _ffn_out(uc, ug, state, x, pe, cw8, cb, fd, png, pgw, ppj, fng, nb, tr, name):
    g, r, f = uc.shape
    d = x.shape[2]
    pdim = pe.shape[2]
    rt = state.shape[1]
    assert r % tr == 0 and tr >= rt
    tok = lambda c: pl.BlockSpec((1, tr, c), lambda a, i: (a, i, 0))
    const = lambda shp: pl.BlockSpec(shp, lambda a, i: tuple(0 for _ in shp))
    return pl.pallas_call(
        functools.partial(_ffn_out_kernel, nb=nb, rt=rt),
        grid=(g, r // tr),
        in_specs=[tok(f), tok(f), pl.BlockSpec((1, rt, f), lambda a, i: (a, 0, 0)), tok(d), tok(pdim),
                  const((SUBLANES, f)), const((1, f)), const((f, d)), const((1, d)), const((d, d)),
                  const((pdim, d)), const((1, d))],
        out_specs=tok(d),
        out_shape=jax.ShapeDtypeStruct((g, r, d), F32),
        scratch_shapes=[pltpu.VMEM((rt, f), F32)],
        compiler_params=_cparams("parallel", "arbitrary"),
        name=name,
    )(uc, ug, state, x, pe, cw8, cb, fd, png, pgw, ppj, fng)


def _pad_rows(w, rows):
    return jnp.pad(w, ((0, rows - w.shape[0]), (0, 0)))


def _pack_pairs(s):
    b = s.shape[0]
    s = s.reshape(b, R_HEADS // 2, 2, R_HEAD_DIM, R_HEAD_DIM).transpose(0, 1, 3, 2, 4)
    return s.reshape(b * (R_HEADS // 2) * R_HEAD_DIM, LANES)


def _unpack_pairs(sp, b):
    s = sp.reshape(b, R_HEADS // 2, R_HEAD_DIM, 2, R_HEAD_DIM).transpose(0, 1, 3, 2, 4)
    return s.reshape(b, R_HEADS, R_HEAD_DIM, R_HEAD_DIM)


def _untile_y(yt, b, steps):
    nchunk = yt.shape[0]
    y = yt.reshape(nchunk, b, R_HEADS // 2, R_HEAD_DIM, 2, R_HEAD_DIM)[..., :steps]
    y = y.transpose(1, 0, 5, 2, 4, 3)
    return y.reshape(b, nchunk * steps, R_WIDTH)


def _layer(x, pe, st, wts, nb):
    g, r, d = x.shape
    n = g * r
    nseq = g * nb
    nsteps = r // nb
    tm = 256
    xf = x.reshape(n, d)

    zqk, zv, zo, zr, zg, zif = _norm_proj(xf, wts["norm1_g"], wts["w_in"], wts["in_widths"], tm, "in_proj")

    tr_m = max(tm, st["mconv"].shape[1])
    if r % tr_m:
        tr_m = r
    qk = _mconv(zqk.reshape(g, r, -1), st["mconv"], wts["m_conv_w"], wts["m_conv_b"], wts["m_qk_scale"],
                nb, tr_m, "mconv")
    to_seq = (lambda a: a.reshape(nseq, nsteps, a.shape[-1])) if nb == 1 else \
        (lambda a: a.reshape(nsteps, nseq, a.shape[-1]).transpose(1, 0, 2))
    hm, mc, mn, mm = _mlstm(to_seq(qk), to_seq(zv), to_seq(zo), to_seq(zif), wts["m_gate_bias"],
                            wts["m_norm_g"], st["mC"], st["mn"], st["mm"], "mlstm")
    hm = hm.reshape(n, -1) if nb == 1 else hm.transpose(1, 0, 2).reshape(n, -1)

    tr_r = max(tm, st["rshift"].shape[1])
    ops = _rprep(zr.reshape(g, r, -1), st["rshift"], wts["r_mix"], wts["r_w2"], wts["r_a2"], wts["r_g2"],
                 wts["r_w0"], wts["r_a0"], wts["r_kk"], wts["r_ka"], nb, tr_r, "rprep")
    rr, rw, rk, rv, ra, rb, rg = ops
    flat = lambda a: a.reshape(n, -1)
    yt, s_new = _rscan([to_seq(flat(o)) for o in (rr, rw, rk, rv, ra, rb)], st["rS"], "rscan")
    y = _untile_y(yt, nseq, min(nsteps, SCAN_CHUNK))
    y = y.reshape(n, -1) if nb == 1 else y.transpose(1, 0, 2).reshape(n, -1)

    x1 = _merge(xf, hm, y, flat(rr), flat(rk), flat(rv), flat(rg), zg, wts["r_rk"], wts["r_ln_g"],
                wts["r_ln_b"], wts["w_branch_m"], wts["w_branch_r"], wts["w_out"], tm, "merge")

    uc, ug = _norm_proj(x1, wts["norm2_g"], wts["f_up"], wts["up_widths"], tm, "ffn_up")
    tr_f = max(tm, st["fconv"].shape[1])
    yout = _ffn_out(uc.reshape(g, r, -1), ug.reshape(g, r, -1), st["fconv"], x1.reshape(g, r, d), pe,
                    wts["f_conv_w"], wts["f_conv_b"], wts["f_down"], wts["ple_norm_g"], wts["ple_gate_w"],
                    wts["ple_proj"], wts["final_norm_g"], nb, tr_f, "ffn_out")
    return yout, dict(zqk=zqk, zr=zr, uc=uc, mC=mc, mn=mn, mm=mm, rS=s_new)


def kernel(x_prompt, x_sample, state_mlstm_conv, state_mlstm_C, state_mlstm_n, state_mlstm_m,
           state_rwkv_shift, state_rwkv_S, state_ffn_conv, p_prompt, p_sample,
           norm1_g, w_in, m_conv_w, m_conv_b, m_i_bias, m_f_bias, m_norm_g, w_branch_m,
           r_mix, r_w0, r_w2, r_a0, r_a2, r_g2, r_kk, r_ka, r_rk, r_ln_g, r_ln_b, w_branch_r,
           w_out, norm2_g, f_up, f_conv_w, f_conv_b, f_down, ple_norm_g, ple_gate_w, ple_proj,
           final_norm_g):
    assert norm1_g.shape[0] == 1, "single trunk layer"
    bp, tp, d = x_prompt.shape
    bs, ts, _ = x_sample.shape
    dff = f_down.shape[1]
    mw2 = 2 * M_WIDTH
    row = lambda a: a.reshape(1, -1).astype(F32)

    wi = w_in[0]
    c_if = 4 * M_WIDTH
    c_r = c_if + 2 * M_HEADS
    c_g = c_r + R_COLS
    w_in_cat = jnp.concatenate(
        [wi[:, :c_if], wi[:, c_r:c_g], wi[:, c_g:], jnp.pad(wi[:, c_if:c_r], ((0, 0), (0, LANES - 2 * M_HEADS)))],
        axis=1).astype(BF16)
    lora = jnp.zeros((LANES, R_WIDTH), F32)
    wts = dict(
        norm1_g=row(norm1_g[0]), w_in=w_in_cat,
        in_widths=(mw2, M_WIDTH, M_WIDTH, R_COLS, 2 * d, LANES),
        m_conv_w=_pad_rows(m_conv_w[0], SUBLANES), m_conv_b=row(m_conv_b[0]),
        m_qk_scale=jnp.concatenate([jnp.ones((1, M_WIDTH), F32),
                                    jnp.full((1, M_WIDTH), M_HEAD_DIM ** -0.5, F32)], axis=1),
        m_gate_bias=jnp.pad(jnp.concatenate([m_i_bias[0], m_f_bias[0]]).reshape(1, -1),
                            ((0, 0), (0, LANES - 2 * M_HEADS))),
        m_norm_g=row(m_norm_g[0]), w_branch_m=w_branch_m[0].astype(BF16),
        r_mix=row(r_mix[0]), r_w0=row(r_w0[0]), r_a0=row(r_a0[0]),
        r_w2=lora.at[:R_DECAY_LORA].set(r_w2[0]).astype(BF16),
        r_a2=lora.at[R_DECAY_LORA:R_DECAY_LORA + R_A_LORA].set(r_a2[0]).astype(BF16),
        r_g2=r_g2[0].astype(BF16), r_kk=row(r_kk[0]), r_ka=row(r_ka[0]), r_rk=row(r_rk[0]),
        r_ln_g=row(r_ln_g[0]), r_ln_b=row(r_ln_b[0]), w_branch_r=w_branch_r[0].astype(BF16),
        w_out=w_out[0].astype(BF16), norm2_g=row(norm2_g[0]), f_up=f_up[0].astype(BF16),
        up_widths=(dff, dff), f_conv_w=_pad_rows(f_conv_w[0], SUBLANES), f_conv_b=row(f_conv_b[0]),
        f_down=f_down[0].astype(BF16), ple_norm_g=row(ple_norm_g[0]), ple_gate_w=ple_gate_w[0].astype(BF16),
        ple_proj=ple_proj[0].astype(BF16), final_norm_g=row(final_norm_g),
    )

    def seq_states(c, nv, m):
        b = c.shape[0]
        nv8 = jnp.pad(nv, ((0, 0), (0, SUBLANES - M_HEADS), (0, 0)))
        m8 = jnp.pad(jnp.broadcast_to(m[:, :, None], (b, M_HEADS, LANES)), ((0, 0), (0, SUBLANES - M_HEADS), (0, 0)))
        return c, nv8, m8

    zc, zn, zm = seq_states(jnp.zeros((bp, M_HEADS, M_HEAD_DIM, M_HEAD_DIM), F32),
                            jnp.zeros((bp, M_HEADS, M_HEAD_DIM), F32), jnp.zeros((bp, M_HEADS), F32))
    st_p = dict(mconv=jnp.zeros((bp, SUBLANES, mw2), F32), rshift=jnp.zeros((bp, SUBLANES, R_COLS), F32),
                fconv=jnp.zeros((bp, SUBLANES, dff), F32), mC=zc, mn=zn, mm=zm,
                rS=jnp.zeros((bp * (R_HEADS // 2) * R_HEAD_DIM, LANES), F32))
    y_p, new_p = _layer(x_prompt, p_prompt[0], st_p, wts, 1)

    tmaj = lambda a: a.transpose(1, 0, 2).reshape(1, a.shape[0] * a.shape[1], a.shape[2])
    sc, sn, sm = seq_states(state_mlstm_C[0], state_mlstm_n[0], state_mlstm_m[0])
    st_s = dict(mconv=tmaj(state_mlstm_conv[0]), rshift=state_rwkv_shift[0][None], fconv=tmaj(state_ffn_conv[0]),
                mC=sc, mn=sn, mm=sm, rS=_pack_pairs(state_rwkv_S[0]))
    y_s, new_s = _layer(tmaj(x_sample), tmaj(p_sample[0]), st_s, wts, bs)
    y_sample = y_s.reshape(ts, bs, d).transpose(1, 0, 2)

    def finish(new, b, t, tmajor):
        if tmajor:
            last = lambda a, k: a.reshape(t, b, -1)[t - k:].transpose(1, 0, 2)
        else:
            last = lambda a, k: a.reshape(b, t, -1)[:, t - k:]
        return (last(new["zqk"], M_CONV - 1)[None], new["mC"][None], new["mn"][:, :M_HEADS][None],
                new["mm"][:, :M_HEADS, 0][None], last(new["zr"], 1)[:, 0][None],
                _unpack_pairs(new["rS"], b)[None], last(new["uc"], F_CONV - 1)[None])

    return (y_p, y_sample) + finish(new_p, bp, tp, False) + finish(new_s, bs, ts, True)
```

```python
import functools

import jax
import jax.numpy as jnp
from jax import lax
from jax.experimental import pallas as pl
from jax.experimental.pallas import tpu as pltpu

F32 = jnp.float32
BF16 = jnp.bfloat16

M_HEADS = 4
M_HEAD_DIM = 128
M_WIDTH = M_HEADS * M_HEAD_DIM
M_CONV = 4
R_HEADS = 8
R_HEAD_DIM = 64
R_WIDTH = R_HEADS * R_HEAD_DIM
R_DECAY_LORA = 64
R_A_LORA = 64
R_GATE_LORA = 128
R_COLS = 3 * R_WIDTH + R_DECAY_LORA + R_A_LORA + R_GATE_LORA
F_CONV = 3
EPS = 1e-6
GN_EPS = 64e-5

LANES = 128
SUBLANES = 8
VMEM_LIMIT_BYTES = 56 * 1024 * 1024

MLSTM_CHUNK = 128
SCAN_CHUNK = 64
SCAN_BATCH_GROUP = 4
SCAN_SA_PIECES = 1


def _cparams(*sem):
    return pltpu.CompilerParams(dimension_semantics=sem, vmem_limit_bytes=VMEM_LIMIT_BYTES)


def _sigmoid(x):
    return 1.0 / (1.0 + jnp.exp(-x))


def _softplus(x):
    return jnp.maximum(x, 0.0) + jnp.log1p(jnp.exp(-jnp.abs(x)))


def _log_sigmoid(x):
    return -_softplus(-x)


def _gelu_tanh(x):
    return 0.5 * x * (1.0 + jnp.tanh(0.7978845608028654 * (x + 0.044715 * (x * x * x))))


def _bdot(a, b):
    return jnp.dot(a.astype(BF16), b.astype(BF16), preferred_element_type=F32)


def _rmsnorm(x, g):
    return x * lax.rsqrt(jnp.mean(x * x, axis=-1, keepdims=True) + EPS) * g


def _split_dot(x, m01, parts):
    acc = None
    r = x
    for p in range(parts):
        h = r.astype(BF16)
        d = jnp.dot(h, m01, preferred_element_type=F32)
        acc = d if acc is None else acc + d
        if p + 1 < parts:
            r = r - h.astype(F32)
    return acc


def _split_dot_lhs01(m01, x, parts):
    acc = None
    r = x
    for p in range(parts):
        h = r.astype(BF16)
        d = jnp.dot(m01, h, preferred_element_type=F32)
        acc = d if acc is None else acc + d
        if p + 1 < parts:
            r = r - h.astype(F32)
    return acc


def _head_ones(n):
    r = lax.broadcasted_iota(jnp.int32, (n, n), 0) // R_HEAD_DIM
    c = lax.broadcasted_iota(jnp.int32, (n, n), 1) // R_HEAD_DIM
    return jnp.where(r == c, 1.0, 0.0).astype(BF16)


def _pair_ones():
    return _head_ones(LANES)


def _head_sum64(x, ones):
    outs = [_split_dot(x[:, s:s + LANES], ones, 3) for s in range(0, x.shape[1], LANES)]
    return outs[0] if len(outs) == 1 else jnp.concatenate(outs, axis=1)


def _shift_rows(x, tail, s):
    n, rt = x.shape[0], tail.shape[0]
    if s % SUBLANES == 0:
        if s == n:
            return tail[rt - s:]
        return jnp.concatenate([tail[rt - s:], x[:n - s]], axis=0)
    assert s < SUBLANES
    r = pltpu.roll(x, s, 0)
    t8 = pltpu.roll(tail[rt - SUBLANES:], s, 0)
    row = lax.broadcasted_iota(jnp.int32, (SUBLANES, x.shape[1]), 0)
    first = jnp.where(row < s, t8, r[:SUBLANES])
    if n == SUBLANES:
        return first
    return jnp.concatenate([first, r[SUBLANES:]], axis=0)


def _norm_proj_kernel(x_ref, g_ref, w_ref, *out_refs, widths):
    h = _rmsnorm(x_ref[...], g_ref[...]).astype(BF16)
    c = 0
    for o_ref, wd in zip(out_refs, widths):
        o_ref[...] = jnp.dot(h, w_ref[:, c:c + wd], preferred_element_type=F32)
        c += wd


def _norm_proj(x, g, w_bf16, widths, tm, name):
    n, d = x.shape
    wtot = w_bf16.shape[1]
    assert sum(widths) == wtot and n % tm == 0
    return pl.pallas_call(
        functools.partial(_norm_proj_kernel, widths=tuple(widths)),
        grid=(n // tm,),
        in_specs=[pl.BlockSpec((tm, d), lambda i: (i, 0)),
                  pl.BlockSpec((1, d), lambda i: (0, 0)),
                  pl.BlockSpec((d, wtot), lambda i: (0, 0))],
        out_specs=[pl.BlockSpec((tm, wd), lambda i: (i, 0)) for wd in widths],
        out_shape=[jax.ShapeDtypeStruct((n, wd), F32) for wd in widths],
        compiler_params=_cparams("parallel"),
        name=name,
    )(x, g, w_bf16)


def _mconv_kernel(x_ref, st_ref, w_ref, b_ref, sc_ref, o_ref, tail_scr, *, nb, rt):
    @pl.when(pl.program_id(1) == 0)
    def _():
        tail_scr[...] = st_ref[0]

    x = x_ref[0]
    tail = tail_scr[...]
    y = b_ref[...] + w_ref[M_CONV - 1:M_CONV, :] * x
    for j in range(M_CONV - 1):
        y = y + w_ref[j:j + 1, :] * _shift_rows(x, tail, (M_CONV - 1 - j) * nb)
    tail_scr[...] = x[x.shape[0] - rt:]
    o_ref[0] = y * _sigmoid(y) * sc_ref[...]


def _mconv(x, state, w8, b, scale, nb, tr, name):
    g, r, c = x.shape
    rt = state.shape[1]
    assert r % tr == 0 and tr >= rt
    return pl.pallas_call(
        functools.partial(_mconv_kernel, nb=nb, rt=rt),
        grid=(g, r // tr),
        in_specs=[pl.BlockSpec((1, tr, c), lambda a, i: (a, i, 0)),
                  pl.BlockSpec((1, rt, c), lambda a, i: (a, 0, 0)),
                  pl.BlockSpec((SUBLANES, c), lambda a, i: (0, 0)),
                  pl.BlockSpec((1, c), lambda a, i: (0, 0)),
                  pl.BlockSpec((1, c), lambda a, i: (0, 0))],
        out_specs=pl.BlockSpec((1, tr, c), lambda a, i: (a, i, 0)),
        out_shape=jax.ShapeDtypeStruct((g, r, c), F32),
        scratch_shapes=[pltpu.VMEM((rt, c), F32)],
        compiler_params=_cparams("parallel", "arbitrary"),
        name=name,
    )(x, state, w8, b, scale)


def _mlstm_kernel(qk_ref, v_ref, og_ref, gt_ref, gb_ref, ng_ref, c0_ref, n0_ref, m0_ref,
                  h_ref, ct_ref, nt_ref, mt_ref, c_scr, n_scr, m_scr, *, lb):
    lp, hd, nh = MLSTM_CHUNK, M_HEAD_DIM, M_HEADS

    @pl.when(pl.program_id(1) == 0)
    def _():
        c_scr[...] = c0_ref[0]
        n_scr[...] = n0_ref[0]
        m_scr[...] = m0_ref[0]

    def pad(x):
        if lb == lp:
            return x
        return jnp.concatenate([x, jnp.zeros((lp - lb, x.shape[1]), x.dtype)], axis=0)

    qk = pad(qk_ref[0])
    v = pad(v_ref[0])
    gates = pad(gt_ref[0]) + gb_ref[...]
    row = lax.broadcasted_iota(jnp.int32, (lp, LANES), 0)
    lane = lax.broadcasted_iota(jnp.int32, (lp, LANES), 1)
    valid = row < lb
    li = jnp.where(valid, gates, -jnp.inf)
    lf = jnp.where(valid & (lane >= nh) & (lane < 2 * nh), _log_sigmoid(gates), 0.0)
    sq_r = lax.broadcasted_iota(jnp.int32, (lp, lp), 0)
    sq_c = lax.broadcasted_iota(jnp.int32, (lp, lp), 1)
    causal = sq_r >= sq_c
    tril = jnp.where(causal, 1.0, 0.0).astype(BF16)
    bcum = _split_dot_lhs01(tril, lf, 3)
    xg = jnp.where(lane < nh, li, bcum)
    xg_t = xg.T

    for h in range(nh):
        bcol = xg[:, nh + h:nh + h + 1]
        icol = xg[:, h:h + 1]
        brow = xg_t[nh + h:nh + h + 1, :]
        irow = xg_t[h:h + 1, :]
        m_prev = m_scr[h:h + 1, :][:, 0:1]
        q = qk[:, h * hd:(h + 1) * hd]
        k = qk[:, M_WIDTH + h * hd:M_WIDTH + (h + 1) * hd]
        vh = v[:, h * hd:(h + 1) * hd]
        qb, kb, vb = q.astype(BF16), k.astype(BF16), vh.astype(BF16)
        c_h = c_scr[h]
        n_h = n_scr[h:h + 1, :]

        dm = jnp.where(causal, bcol - brow + irow, -jnp.inf)
        inter = bcol + m_prev
        mt = jnp.maximum(inter, jnp.max(dm, axis=1, keepdims=True))
        s = lax.dot_general(qb, kb, (((1,), (1,)), ((), ())), preferred_element_type=F32)
        p = jnp.exp(dm - mt) * s
        sc = jnp.exp(inter - mt)
        num = sc * jnp.dot(qb, c_h.astype(BF16), preferred_element_type=F32) \
            + jnp.dot(p.astype(BF16), vb, preferred_element_type=F32)
        den = sc * jnp.sum(q * n_h, axis=1, keepdims=True) + jnp.sum(p, axis=1, keepdims=True)
        hh = num / jnp.maximum(jnp.abs(den), jnp.exp(-mt))
        mu = jnp.mean(hh, axis=1, keepdims=True)
        d = hh - mu
        y = d * lax.rsqrt(jnp.mean(d * d, axis=1, keepdims=True) + EPS) * ng_ref[:, h * hd:(h + 1) * hd]
        h_ref[0, :, h * hd:(h + 1) * hd] = _sigmoid(og_ref[0, :, h * hd:(h + 1) * hd]) * y[:lb]

        btot = bcol[lp - 1:lp, :]
        wlog = btot - bcol + icol
        m_new = jnp.maximum(btot + m_prev, jnp.max(wlog, axis=0, keepdims=True))
        s0 = jnp.exp(btot + m_prev - m_new)
        wk = jnp.exp(wlog - m_new) * k
        c_scr[h] = s0 * c_h + jnp.dot(wk.T.astype(BF16), vb, preferred_element_type=F32)
        n_scr[h:h + 1, :] = s0 * n_h + jnp.sum(wk, axis=0, keepdims=True)
        m_scr[h:h + 1, :] = jnp.broadcast_to(m_new, (1, LANES))

    @pl.when(pl.program_id(1) == pl.num_programs(1) - 1)
    def _():
        ct_ref[0] = c_scr[...]
        nt_ref[0] = n_scr[...]
        mt_ref[0] = m_scr[...]


def _mlstm(qk, v, og, gt, gbias, ng, c0, n0, m0, name):
    b, t, _ = qk.shape
    lb = min(t, MLSTM_CHUNK)
    assert t % lb == 0
    nc = t // lb
    tok = lambda w: pl.BlockSpec((1, lb, w), lambda a, c: (a, c, 0))
    const = lambda shp: pl.BlockSpec(shp, lambda a, c: tuple(0 for _ in shp))
    st4 = pl.BlockSpec((1, M_HEADS, M_HEAD_DIM, M_HEAD_DIM), lambda a, c: (a, 0, 0, 0))
    st3 = pl.BlockSpec((1, SUBLANES, LANES), lambda a, c: (a, 0, 0))
    return pl.pallas_call(
        functools.partial(_mlstm_kernel, lb=lb),
        grid=(b, nc),
        in_specs=[tok(2 * M_WIDTH), tok(M_WIDTH), tok(M_WIDTH), tok(LANES),
                  const((1, LANES)), const((1, M_WIDTH)), st4, st3, st3],
        out_specs=[tok(M_WIDTH), st4, st3, st3],
        out_shape=[jax.ShapeDtypeStruct((b, t, M_WIDTH), F32),
                   jax.ShapeDtypeStruct((b, M_HEADS, M_HEAD_DIM, M_HEAD_DIM), F32),
                   jax.ShapeDtypeStruct((b, SUBLANES, LANES), F32),
                   jax.ShapeDtypeStruct((b, SUBLANES, LANES), F32)],
        scratch_shapes=[pltpu.VMEM((M_HEADS, M_HEAD_DIM, M_HEAD_DIM), F32),
                        pltpu.VMEM((SUBLANES, LANES), F32),
                        pltpu.VMEM((SUBLANES, LANES), F32)],
        compiler_params=_cparams("parallel", "arbitrary"),
        name=name,
    )(qk, v, og, gt, gbias, ng, c0, n0, m0)


def _rprep_kernel(z_ref, st_ref, mix_ref, w2_ref, a2_ref, g2_ref, w0_ref, a0_ref, kk_ref, ka_ref,
                  r_o, w_o, k_o, v_o, a_o, b_o, g_o, tail_scr, *, nb, rt):
    @pl.when(pl.program_id(1) == 0)
    def _():
        tail_scr[...] = st_ref[0]

    z = z_ref[0]
    prev = _shift_rows(z, tail_scr[...], nb)
    tail_scr[...] = z[z.shape[0] - rt:]
    xm = z + (prev - z) * mix_ref[...]
    w = R_WIDTH
    r, kr, vr = xm[:, 0:w], xm[:, w:2 * w], xm[:, 2 * w:3 * w]
    lora_in = xm[:, 3 * w:3 * w + LANES]
    gd = xm[:, 3 * w + LANES:3 * w + 2 * LANES]
    lw = _bdot(jnp.tanh(lora_in), w2_ref[...])
    la = _bdot(lora_in, a2_ref[...])
    wlog = -_softplus(-(w0_ref[...] + lw)) - 0.5
    a = _sigmoid(a0_ref[...] + la)
    ones = _pair_ones()
    kk = kr * kk_ref[...]
    kk = kk / jnp.maximum(jnp.sqrt(_head_sum64(kk * kk, ones)), 1e-12)
    r_o[0] = r
    w_o[0] = jnp.exp(-jnp.exp(wlog))
    k_o[0] = kr * (1.0 + (a - 1.0) * ka_ref[...])
    v_o[0] = vr
    a_o[0] = -kk
    b_o[0] = kk * a
    g_o[0] = _bdot(_sigmoid(gd), g2_ref[...])


def _rprep(z, state, mix, w2p, a2p, g2, w0, a0, kkw, kaw, nb, tr, name):
    g, r, c = z.shape
    rt = state.shape[1]
    assert r % tr == 0 and tr >= rt
    w = R_WIDTH
    const = lambda shp: pl.BlockSpec(shp, lambda a, i: tuple(0 for _ in shp))
    tok = pl.BlockSpec((1, tr, w), lambda a, i: (a, i, 0))
    return pl.pallas_call(
        functools.partial(_rprep_kernel, nb=nb, rt=rt),
        grid=(g, r // tr),
        in_specs=[pl.BlockSpec((1, tr, c), lambda a, i: (a, i, 0)),
                  pl.BlockSpec((1, rt, c), lambda a, i: (a, 0, 0)),
                  const((1, c)), const((LANES, w)), const((LANES, w)), const((LANES, w)),
                  const((1, w)), const((1, w)), const((1, w)), const((1, w))],
        out_specs=[tok] * 7,
        out_shape=[jax.ShapeDtypeStruct((g, r, w), F32)] * 7,
        scratch_shapes=[pltpu.VMEM((rt, c), F32)],
        compiler_params=_cparams("parallel", "arbitrary"),
        name=name,
    )(z, state, mix, w2p, a2p, g2, w0, a0, kkw, kaw)


def _rscan_kernel(r_ref, w_ref, k_ref, v_ref, a_ref, b_ref, s0_ref, y_ref, st_ref, s_scr, y_scr, *, bg, steps):
    hp = R_HEADS // 2
    npair = bg * hp
    nquad = npair // 2
    hd = R_HEAD_DIM
    tc = pl.program_id(1)
    pairs = [(bi, p) for bi in range(bg) for p in range(hp)]
    sl = lambda q: slice(q * hd, (q + 1) * hd)

    @pl.when(tc == 0)
    def _():
        for q, (bi, p) in enumerate(pairs):
            s_scr[sl(q), :] = jnp.concatenate([s0_ref[bi, 2 * p], s0_ref[bi, 2 * p + 1]], axis=1)

    y_scr[...] = jnp.zeros(y_scr.shape, F32)
    ones = _head_ones(2 * LANES)
    lane = lax.broadcasted_iota(jnp.int32, (hd, LANES), 1) & (hd - 1)
    diag = lane == lax.broadcasted_iota(jnp.int32, (hd, LANES), 0)

    def side_by_side(xs):
        return jnp.concatenate([jnp.concatenate([xs[2 * u], xs[2 * u + 1]], axis=1) for u in range(nquad)], axis=0)

    def pair_of(res, q):
        return res[(q // 2) * hd:(q // 2 + 1) * hd, (q % 2) * LANES:(q % 2 + 1) * LANES]

    def step8(t8, carry):
        tiles = [[ref[bi, t8, :, p * LANES:(p + 1) * LANES] for bi, p in pairs]
                 for ref in (r_ref, w_ref, k_ref, v_ref, a_ref, b_ref)]
        s_cur = [s_scr[sl(q), :] for q in range(npair)]
        for j in range(SUBLANES):
            r_t, w_t, k_t, v_t, a_t, b_t = [[tl[q][j:j + 1, :] for q in range(npair)] for tl in tiles]
            prod = side_by_side([s_cur[q] * a_t[q] for q in range(npair)])
            sa = _split_dot(prod, ones, SCAN_SA_PIECES)
            vdiag = side_by_side([jnp.where(diag, v_t[q], 0.0) for q in range(npair)])
            vcol = jnp.dot(vdiag.astype(BF16), ones, preferred_element_type=F32)
            s_cur = [s_cur[q] * w_t[q] + pair_of(sa, q) * b_t[q] + pair_of(vcol, q) * k_t[q] for q in range(npair)]
            outs = side_by_side([s_cur[q] * r_t[q] for q in range(npair)])
            yb = jnp.dot(outs.astype(BF16), ones, preferred_element_type=F32)
            hit = lane == t8 * SUBLANES + j
            for q in range(npair):
                pltpu.store(y_scr.at[sl(q), :], pair_of(yb, q), mask=hit)
        for q in range(npair):
            s_scr[sl(q), :] = s_cur[q]
        return carry

    lax.fori_loop(0, steps // SUBLANES, step8, 0)

    low = lax.broadcasted_iota(jnp.int32, (hd, LANES), 1) < hd
    for u in range(nquad):
        yt = y_scr[u * LANES:(u + 1) * LANES, :].T
        top, bot = yt[:hd], yt[hd:]
        bi, p0 = pairs[2 * u]
        y_ref[bi, :, p0 * LANES:(p0 + 1) * LANES] = jnp.where(low, top, pltpu.roll(bot, hd, 1))[:steps]
        y_ref[bi, :, (p0 + 1) * LANES:(p0 + 2) * LANES] = jnp.where(low, pltpu.roll(top, hd, 1), bot)[:steps]

    @pl.when(tc == pl.num_programs(1) - 1)
    def _():
        for q, (bi, p) in enumerate(pairs):
            s_fin = s_scr[sl(q), :]
            st_ref[bi, 2 * p] = s_fin[:, :hd]
            st_ref[bi, 2 * p + 1] = s_fin[:, hd:]


def _rscan(ops, s0, name):
    nbatch, nsteps, _ = ops[0].shape
    bg = SCAN_BATCH_GROUP
    steps = min(nsteps, SCAN_CHUNK)
    assert nbatch % bg == 0 and nsteps % steps == 0 and steps % SUBLANES == 0
    rows = bg * (R_HEADS // 2) * R_HEAD_DIM
    ops = [o.reshape(nbatch, nsteps // SUBLANES, SUBLANES, R_WIDTH) for o in ops]
    opspec = pl.BlockSpec((bg, steps // SUBLANES, SUBLANES, R_WIDTH), lambda g, c: (g, c, 0, 0))
    stspec = pl.BlockSpec((bg, R_HEADS, R_HEAD_DIM, R_HEAD_DIM), lambda g, c: (g, 0, 0, 0))
    return pl.pallas_call(
        functools.partial(_rscan_kernel, bg=bg, steps=steps),
        grid=(nbatch // bg, nsteps // steps),
        in_specs=[opspec] * 6 + [stspec],
        out_specs=[pl.BlockSpec((bg, steps, R_WIDTH), lambda g, c: (g, c, 0)), stspec],
        out_shape=[jax.ShapeDtypeStruct((nbatch, nsteps, R_WIDTH), F32),
                   jax.ShapeDtypeStruct(s0.shape, F32)],
        scratch_shapes=[pltpu.VMEM((rows, LANES), F32), pltpu.VMEM((rows, LANES), F32)],
        compiler_params=_cparams("parallel", "arbitrary"),
        name=name,
    )(*ops, s0)


def _merge_kernel(x_ref, hm_ref, y_ref, r_ref, k_ref, v_ref, g_ref, zg_ref, rk_ref, lng_ref, lnb_ref,
                  wbm_ref, wbr_ref, wo_ref, o_ref):
    ones = _pair_ones()
    y = y_ref[...]
    inv = 1.0 / R_HEAD_DIM
    mu = _head_sum64(y, ones) * inv
    d = y - mu
    var = _head_sum64(d * d, ones) * inv
    yn = d * lax.rsqrt(var + GN_EPS) * lng_ref[...] + lnb_ref[...]
    v = v_ref[...]
    bonus = _head_sum64(r_ref[...] * k_ref[...] * rk_ref[...], ones) * v
    yr = (yn + bonus) * g_ref[...]
    y_b = _bdot(yr, wbr_ref[...])
    y_a = _bdot(hm_ref[...], wbm_ref[...])
    dm = x_ref.shape[1]
    merged = _sigmoid(zg_ref[:, 0:dm]) * y_a + _sigmoid(zg_ref[:, dm:2 * dm]) * y_b
    o_ref[...] = x_ref[...] + _bdot(merged, wo_ref[...])


def _merge(x, hm, y, r, k, v, g, zg, rk, lng, lnb, wbm, wbr, wo, tm, name):
    n, d = x.shape
    w = R_WIDTH
    tok = lambda c: pl.BlockSpec((tm, c), lambda i: (i, 0))
    const = lambda shp: pl.BlockSpec(shp, lambda i: tuple(0 for _ in shp))
    return pl.pallas_call(
        _merge_kernel,
        grid=(n // tm,),
        in_specs=[tok(d), tok(M_WIDTH), tok(w), tok(w), tok(w), tok(w), tok(w), tok(2 * d),
                  const((1, w)), const((1, w)), const((1, w)),
                  const((M_WIDTH, d)), const((w, d)), const((d, d))],
        out_specs=tok(d),
        out_shape=jax.ShapeDtypeStruct((n, d), F32),
        compiler_params=_cparams("parallel"),
        name=name,
    )(x, hm, y, r, k, v, g, zg, rk, lng, lnb, wbm, wbr, wo)


def _ffn_out_kernel(uc_ref, ug_ref, st_ref, x_ref, pe_ref, cw_ref, cb_ref, fd_ref, png_ref, pgw_ref,
                    ppj_ref, fng_ref, y_ref, tail_scr, *, nb, rt):
    @pl.when(pl.program_id(1) == 0)
    def _():
        tail_scr[...] = st_ref[0]

    uc = uc_ref[0]
    tail = tail_scr[...]
    c = cb_ref[...] + cw_ref[F_CONV - 1:F_CONV, :] * uc
    for j in range(F_CONV - 1):
        c = c + cw_ref[j:j + 1, :] * _shift_rows(uc, tail, (F_CONV - 1 - j) * nb)
    tail_scr[...] = uc[uc.shape[0] - rt:]
    x2 = x_ref[0] + _bdot(_gelu_tanh(c) * ug_ref[0], fd_ref[...])
    gate = _sigmoid(_bdot(_rmsnorm(x2, png_ref[...]), pgw_ref[...]))
    x3 = x2 + gate * _bdot(pe_ref[0], ppj_ref[...])
    y_ref[0] = _rmsnorm(x3, fng_ref[...])


def _ffn_out(uc, ug, state, x, pe, cw8, cb, fd, png, pgw, ppj, fng, nb, tr, name):
    g, r, f = uc.shape
    d = x.shape[2]
    pdim = pe.shape[2]
    rt = state.shape[1]
    assert r % tr == 0 and tr >= rt
    tok = lambda c: pl.BlockSpec((1, tr, c), lambda a, i: (a, i, 0))
    const = lambda shp: pl.BlockSpec(shp, lambda a, i: tuple(0 for _ in shp))
    return pl.pallas_call(
        functools.partial(_ffn_out_kernel, nb=nb, rt=rt),
        grid=(g, r // tr),
        in_specs=[tok(f), tok(f), pl.BlockSpec((1, rt, f), lambda a, i: (a, 0, 0)), tok(d), tok(pdim),
                  const((SUBLANES, f)), const((1, f)), const((f, d)), const((1, d)), const((d, d)),
                  const((pdim, d)), const((1, d))],
        out_specs=tok(d),
        out_shape=jax.ShapeDtypeStruct((g, r, d), F32),
        scratch_shapes=[pltpu.VMEM((rt, f), F32)],
        compiler_params=_cparams("parallel", "arbitrary"),
        name=name,
    )(uc, ug, state, x, pe, cw8, cb, fd, png, pgw, ppj, fng)


def _pad_rows(w, rows):
    return jnp.pad(w, ((0, rows - w.shape[0]), (0, 0)))


def _layer(x, pe, st, wts, nb):
    g, r, d = x.shape
    n = g * r
    nseq = g * nb
    nsteps = r // nb
    tm = 256
    xf = x.reshape(n, d)

    zqk, zv, zo, zr, zg, zif = _norm_proj(xf, wts["norm1_g"], wts["w_in"], wts["in_widths"], tm, "in_proj")

    tr_m = max(tm, st["mconv"].shape[1])
    if r % tr_m:
        tr_m = r
    qk = _mconv(zqk.reshape(g, r, -1), st["mconv"], wts["m_conv_w"], wts["m_conv_b"], wts["m_qk_scale"],
                nb, tr_m, "mconv")
    to_seq = (lambda a: a.reshape(nseq, nsteps, a.shape[-1])) if nb == 1 else \
        (lambda a: a.reshape(nsteps, nseq, a.shape[-1]).transpose(1, 0, 2))
    hm, mc, mn, mm = _mlstm(to_seq(qk), to_seq(zv), to_seq(zo), to_seq(zif), wts["m_gate_bias"],
                            wts["m_norm_g"], st["mC"], st["mn"], st["mm"], "mlstm")
    hm = hm.reshape(n, -1) if nb == 1 else hm.transpose(1, 0, 2).reshape(n, -1)

    tr_r = max(tm, st["rshift"].shape[1])
    ops = _rprep(zr.reshape(g, r, -1), st["rshift"], wts["r_mix"], wts["r_w2"], wts["r_a2"], wts["r_g2"],
                 wts["r_w0"], wts["r_a0"], wts["r_kk"], wts["r_ka"], nb, tr_r, "rprep")
    rr, rw, rk, rv, ra, rb, rg = ops
    flat = lambda a: a.reshape(n, -1)
    y, s_new = _rscan([to_seq(flat(o)) for o in (rr, rw, rk, rv, ra, rb)], st["rS"], "rscan")
    y = y.reshape(n, -1) if nb == 1 else y.transpose(1, 0, 2).reshape(n, -1)

    x1 = _merge(xf, hm, y, flat(rr), flat(rk), flat(rv), flat(rg), zg, wts["r_rk"], wts["r_ln_g"],
                wts["r_ln_b"], wts["w_branch_m"], wts["w_branch_r"], wts["w_out"], tm, "merge")

    uc, ug = _norm_proj(x1, wts["norm2_g"], wts["f_up"], wts["up_widths"], tm, "ffn_up")
    tr_f = max(tm, st["fconv"].shape[1])
    yout = _ffn_out(uc.reshape(g, r, -1), ug.reshape(g, r, -1), st["fconv"], x1.reshape(g, r, d), pe,
                    wts["f_conv_w"], wts["f_conv_b"], wts["f_down"], wts["ple_norm_g"], wts["ple_gate_w"],
                    wts["ple_proj"], wts["final_norm_g"], nb, tr_f, "ffn_out")
    return yout, dict(zqk=zqk, zr=zr, uc=uc, mC=mc, mn=mn, mm=mm, rS=s_new)


def kernel(x_prompt, x_sample, state_mlstm_conv, state_mlstm_C, state_mlstm_n, state_mlstm_m,
           state_rwkv_shift, state_rwkv_S, state_ffn_conv, p_prompt, p_sample,
           norm1_g, w_in, m_conv_w, m_conv_b, m_i_bias, m_f_bias, m_norm_g, w_branch_m,
           r_mix, r_w0, r_w2, r_a0, r_a2, r_g2, r_kk, r_ka, r_rk, r_ln_g, r_ln_b, w_branch_r,
           w_out, norm2_g, f_up, f_conv_w, f_conv_b, f_down, ple_norm_g, ple_gate_w, ple_proj,
           final_norm_g):
    assert norm1_g.shape[0] == 1, "single trunk layer"
    bp, tp, d = x_prompt.shape
    bs, ts, _ = x_sample.shape
    dff = f_down.shape[1]
    mw2 = 2 * M_WIDTH
    row = lambda a: a.reshape(1, -1).astype(F32)

    wi = w_in[0]
    c_if = 4 * M_WIDTH
    c_r = c_if + 2 * M_HEADS
    c_g = c_r + R_COLS
    w_in_cat = jnp.concatenate(
        [wi[:, :c_if], wi[:, c_r:c_g], wi[:, c_g:], jnp.pad(wi[:, c_if:c_r], ((0, 0), (0, LANES - 2 * M_HEADS)))],
        axis=1).astype(BF16)
    lora = jnp.zeros((LANES, R_WIDTH), F32)
    wts = dict(
        norm1_g=row(norm1_g[0]), w_in=w_in_cat,
        in_widths=(mw2, M_WIDTH, M_WIDTH, R_COLS, 2 * d, LANES),
        m_conv_w=_pad_rows(m_conv_w[0], SUBLANES), m_conv_b=row(m_conv_b[0]),
        m_qk_scale=jnp.concatenate([jnp.ones((1, M_WIDTH), F32),
                                    jnp.full((1, M_WIDTH), M_HEAD_DIM ** -0.5, F32)], axis=1),
        m_gate_bias=jnp.pad(jnp.concatenate([m_i_bias[0], m_f_bias[0]]).reshape(1, -1),
                            ((0, 0), (0, LANES - 2 * M_HEADS))),
        m_norm_g=row(m_norm_g[0]), w_branch_m=w_branch_m[0].astype(BF16),
        r_mix=row(r_mix[0]), r_w0=row(r_w0[0]), r_a0=row(r_a0[0]),
        r_w2=lora.at[:R_DECAY_LORA].set(r_w2[0]).astype(BF16),
        r_a2=lora.at[R_DECAY_LORA:R_DECAY_LORA + R_A_LORA].set(r_a2[0]).astype(BF16),
        r_g2=r_g2[0].astype(BF16), r_kk=row(r_kk[0]), r_ka=row(r_ka[0]), r_rk=row(r_rk[0]),
        r_ln_g=row(r_ln_g[0]), r_ln_b=row(r_ln_b[0]), w_branch_r=w_branch_r[0].astype(BF16),
        w_out=w_out[0].astype(BF16), norm2_g=row(norm2_g[0]), f_up=f_up[0].astype(BF16),
        up_widths=(dff, dff), f_conv_w=_pad_rows(f_conv_w[0], SUBLANES), f_conv_b=row(f_conv_b[0]),
        f_down=f_down[0].astype(BF16), ple_norm_g=row(ple_norm_g[0]), ple_gate_w=ple_gate_w[0].astype(BF16),
        ple_proj=ple_proj[0].astype(BF16), final_norm_g=row(final_norm_g),
    )

    def seq_states(c, nv, m):
        b = c.shape[0]
        nv8 = jnp.pad(nv, ((0, 0), (0, SUBLANES - M_HEADS), (0, 0)))
        m8 = jnp.pad(jnp.broadcast_to(m[:, :, None], (b, M_HEADS, LANES)), ((0, 0), (0, SUBLANES - M_HEADS), (0, 0)))
        return c, nv8, m8

    zc, zn, zm = seq_states(jnp.zeros((bp, M_HEADS, M_HEAD_DIM, M_HEAD_DIM), F32),
                            jnp.zeros((bp, M_HEADS, M_HEAD_DIM), F32), jnp.zeros((bp, M_HEADS), F32))
    st_p = dict(mconv=jnp.zeros((bp, SUBLANES, mw2), F32), rshift=jnp.zeros((bp, SUBLANES, R_COLS), F32),
                fconv=jnp.zeros((bp, SUBLANES, dff), F32), mC=zc, mn=zn, mm=zm,
                rS=jnp.zeros((bp, R_HEADS, R_HEAD_DIM, R_HEAD_DIM), F32))
    y_p, new_p = _layer(x_prompt, p_prompt[0], st_p, wts, 1)

    tmaj = lambda a: a.transpose(1, 0, 2).reshape(1, a.shape[0] * a.shape[1], a.shape[2])
    sc, sn, sm = seq_states(state_mlstm_C[0], state_mlstm_n[0], state_mlstm_m[0])
    st_s = dict(mconv=tmaj(state_mlstm_conv[0]), rshift=state_rwkv_shift[0][None], fconv=tmaj(state_ffn_conv[0]),
                mC=sc, mn=sn, mm=sm, rS=state_rwkv_S[0])
    y_s, new_s = _layer(tmaj(x_sample), tmaj(p_sample[0]), st_s, wts, bs)
    y_sample = y_s.reshape(ts, bs, d).transpose(1, 0, 2)

    def finish(new, b, t, tmajor):
        if tmajor:
            last = lambda a, k: a.reshape(t, b, -1)[t - k:].transpose(1, 0, 2)
        else:
            last = lambda a, k: a.reshape(b, t, -1)[:, t - k:]
        return (last(new["zqk"], M_CONV - 1)[None], new["mC"][None], new["mn"][:, :M_HEADS][None],
                new["mm"][:, :M_HEADS, 0][None], last(new["zr"], 1)[:, 0][None],
                new["rS"][None], last(new["uc"], F_CONV - 1)[None])

    return (y_p, y_sample) + finish(new_p, bp, tp, False) + finish(new_s, bs, ts, True)
```

```python
import functools

import jax
import jax.numpy as jnp
from jax import lax
from jax.experimental import pallas as pl
from jax.experimental.pallas import tpu as pltpu

F32 = jnp.float32
BF16 = jnp.bfloat16

M_HEADS = 4
M_HEAD_DIM = 128
M_WIDTH = M_HEADS * M_HEAD_DIM
M_CONV = 4
R_HEADS = 8
R_HEAD_DIM = 64
R_WIDTH = R_HEADS * R_HEAD_DIM
R_DECAY_LORA = 64
R_A_LORA = 64
R_GATE_LORA = 128
R_COLS = 3 * R_WIDTH + R_DECAY_LORA + R_A_LORA + R_GATE_LORA
F_CONV = 3
EPS = 1e-6
GN_EPS = 64e-5

LANES = 128
SUBLANES = 8
VMEM_LIMIT_BYTES = 56 * 1024 * 1024

FFN_ROWS = 512
MLSTM_CHUNK = 128
MLSTM_SUBCHUNKS = 4
SCAN_CHUNK = 64
SCAN_BATCH_GROUP = 4
SCAN_SA_PIECES = 1


def _cparams(*sem):
    return pltpu.CompilerParams(dimension_semantics=sem, vmem_limit_bytes=VMEM_LIMIT_BYTES)


def _sigmoid(x):
    return 1.0 / (1.0 + jnp.exp(-x))


def _softplus(x):
    return jnp.maximum(x, 0.0) + jnp.log1p(jnp.exp(-jnp.abs(x)))


def _log_sigmoid(x):
    return -_softplus(-x)


def _gelu_tanh(x):
    return 0.5 * x * (1.0 + jnp.tanh(0.7978845608028654 * (x + 0.044715 * (x * x * x))))


def _bdot(a, b):
    return jnp.dot(a.astype(BF16), b.astype(BF16), preferred_element_type=F32)


def _rmsnorm(x, g):
    return x * lax.rsqrt(jnp.mean(x * x, axis=-1, keepdims=True) + EPS) * g


def _split_dot(x, m01, parts):
    acc = None
    r = x
    for p in range(parts):
        h = r.astype(BF16)
        d = jnp.dot(h, m01, preferred_element_type=F32)
        acc = d if acc is None else acc + d
        if p + 1 < parts:
            r = r - h.astype(F32)
    return acc


def _split_dot_lhs01(m01, x, parts):
    acc = None
    r = x
    for p in range(parts):
        h = r.astype(BF16)
        d = jnp.dot(m01, h, preferred_element_type=F32)
        acc = d if acc is None else acc + d
        if p + 1 < parts:
            r = r - h.astype(F32)
    return acc


_NN = (((1,), (0,)), ((), ()))
_NT = (((1,), (1,)), ((), ()))


def _dot_pieces(a, b, dims):
    a_hi, b_hi = a.astype(BF16), b.astype(BF16)
    a_lo, b_lo = (a - a_hi.astype(F32)).astype(BF16), (b - b_hi.astype(F32)).astype(BF16)
    dot = lambda x, y: lax.dot_general(x, y, dims, preferred_element_type=F32)
    return dot(a_hi, b_hi) + dot(a_hi, b_lo) + dot(a_lo, b_hi)


def _head_ones(n):
    r = lax.broadcasted_iota(jnp.int32, (n, n), 0) // R_HEAD_DIM
    c = lax.broadcasted_iota(jnp.int32, (n, n), 1) // R_HEAD_DIM
    return jnp.where(r == c, 1.0, 0.0).astype(BF16)


def _pair_ones():
    return _head_ones(LANES)


def _head_sum64(x, ones):
    outs = [_split_dot(x[:, s:s + LANES], ones, 3) for s in range(0, x.shape[1], LANES)]
    return outs[0] if len(outs) == 1 else jnp.concatenate(outs, axis=1)


def _shift_rows(x, tail, s):
    n, rt = x.shape[0], tail.shape[0]
    if s % SUBLANES == 0:
        if s == n:
            return tail[rt - s:]
        return jnp.concatenate([tail[rt - s:], x[:n - s]], axis=0)
    assert s < SUBLANES
    r = pltpu.roll(x, s, 0)
    t8 = pltpu.roll(tail[rt - SUBLANES:], s, 0)
    row = lax.broadcasted_iota(jnp.int32, (SUBLANES, x.shape[1]), 0)
    first = jnp.where(row < s, t8, r[:SUBLANES])
    if n == SUBLANES:
        return first
    return jnp.concatenate([first, r[SUBLANES:]], axis=0)


def _norm_proj_kernel(x_ref, g_ref, w_ref, *out_refs, widths):
    h = _rmsnorm(x_ref[...], g_ref[...]).astype(BF16)
    c = 0
    for o_ref, wd in zip(out_refs, widths):
        o_ref[...] = jnp.dot(h, w_ref[:, c:c + wd], preferred_element_type=F32).astype(o_ref.dtype)
        c += wd


def _norm_proj(x, g, w_bf16, widths, dtypes, tm, name):
    n, d = x.shape
    wtot = w_bf16.shape[1]
    assert sum(widths) == wtot and n % tm == 0
    return pl.pallas_call(
        functools.partial(_norm_proj_kernel, widths=tuple(widths)),
        grid=(n // tm,),
        in_specs=[pl.BlockSpec((tm, d), lambda i: (i, 0)),
                  pl.BlockSpec((1, d), lambda i: (0, 0)),
                  pl.BlockSpec((d, wtot), lambda i: (0, 0))],
        out_specs=[pl.BlockSpec((tm, wd), lambda i: (i, 0)) for wd in widths],
        out_shape=[jax.ShapeDtypeStruct((n, wd), dt) for wd, dt in zip(widths, dtypes)],
        compiler_params=_cparams("parallel"),
        name=name,
    )(x, g, w_bf16)


def _mconv_kernel(x_ref, st_ref, w_ref, b_ref, sc_ref, o_ref, tail_scr, *, nb, rt):
    @pl.when(pl.program_id(1) == 0)
    def _():
        tail_scr[...] = st_ref[0]

    x = x_ref[0]
    tail = tail_scr[...]
    y = b_ref[...] + w_ref[M_CONV - 1:M_CONV, :] * x
    for j in range(M_CONV - 1):
        y = y + w_ref[j:j + 1, :] * _shift_rows(x, tail, (M_CONV - 1 - j) * nb)
    tail_scr[...] = x[x.shape[0] - rt:]
    o_ref[0] = y * _sigmoid(y) * sc_ref[...]


def _mconv(x, state, w8, b, scale, nb, tr, name):
    g, r, c = x.shape
    rt = state.shape[1]
    assert r % tr == 0 and tr >= rt
    return pl.pallas_call(
        functools.partial(_mconv_kernel, nb=nb, rt=rt),
        grid=(g, r // tr),
        in_specs=[pl.BlockSpec((1, tr, c), lambda a, i: (a, i, 0)),
                  pl.BlockSpec((1, rt, c), lambda a, i: (a, 0, 0)),
                  pl.BlockSpec((SUBLANES, c), lambda a, i: (0, 0)),
                  pl.BlockSpec((1, c), lambda a, i: (0, 0)),
                  pl.BlockSpec((1, c), lambda a, i: (0, 0))],
        out_specs=pl.BlockSpec((1, tr, c), lambda a, i: (a, i, 0)),
        out_shape=jax.ShapeDtypeStruct((g, r, c), F32),
        scratch_shapes=[pltpu.VMEM((rt, c), F32)],
        compiler_params=_cparams("parallel", "arbitrary"),
        name=name,
    )(x, state, w8, b, scale)


def _mlstm_chunk(qk, v, gates, ng_ref, state, lb):
    lp, hd, nh = MLSTM_CHUNK, M_HEAD_DIM, M_HEADS
    row = lax.broadcasted_iota(jnp.int32, (lp, LANES), 0)
    lane = lax.broadcasted_iota(jnp.int32, (lp, LANES), 1)
    valid = row < lb
    li = jnp.where(valid, gates, -jnp.inf)
    lf = jnp.where(valid & (lane >= nh) & (lane < 2 * nh), _log_sigmoid(gates), 0.0)
    sq_r = lax.broadcasted_iota(jnp.int32, (lp, lp), 0)
    sq_c = lax.broadcasted_iota(jnp.int32, (lp, lp), 1)
    tril = jnp.where(sq_r >= sq_c, 1.0, 0.0).astype(BF16)
    bcum = _split_dot_lhs01(tril, lf, 3)
    xg = jnp.where(lane < nh, li, bcum)
    xg_t = xg.T

    keep = sq_r <= sq_c
    ct, n, m = state
    outs, ct_new, n_new, m_new = [], [], [], []
    for h in range(nh):
        brow = xg_t[nh + h:nh + h + 1, :]
        irow = xg_t[h:h + 1, :]
        ccol = xg[:, h:h + 1] - xg[:, nh + h:nh + h + 1]
        q = qk[:, h * hd:(h + 1) * hd]
        k = qk[:, M_WIDTH + h * hd:M_WIDTH + (h + 1) * hd]
        qb, kb = q.astype(BF16), k.astype(BF16)
        v_t = v[:, h * hd:(h + 1) * hd].T

        dm = jnp.where(keep, brow + ccol, -jnp.inf)
        inter = brow + m[h]
        mt = jnp.maximum(inter, jnp.max(dm, axis=0, keepdims=True))
        s_t = lax.dot_general(kb, qb, _NT, preferred_element_type=F32)
        p_t = jnp.exp(dm - mt) * s_t
        sc = jnp.exp(inter - mt)
        num = sc * lax.dot_general(ct[h].astype(BF16), qb, _NT, preferred_element_type=F32) \
            + jnp.dot(v_t.astype(BF16), p_t.astype(BF16), preferred_element_type=F32)
        qn = _dot_pieces(jnp.broadcast_to(n[h], (SUBLANES, hd)), q, _NT)[0:1, :]
        den = sc * qn + jnp.sum(p_t, axis=0, keepdims=True)
        hh = num / jnp.maximum(jnp.abs(den), jnp.exp(-mt))
        mu = jnp.mean(hh, axis=0, keepdims=True)
        d = hh - mu
        outs.append(d * lax.rsqrt(jnp.mean(d * d, axis=0, keepdims=True) + EPS) * ng_ref[h])

        btot = brow[:, lp - 1:lp]
        wlog = btot - brow + irow
        m_h = jnp.maximum(btot + m[h], jnp.max(wlog, axis=1, keepdims=True))
        s0 = jnp.exp(btot + m[h] - m_h)
        ws = jnp.exp(wlog - m_h)
        ct_new.append(s0 * ct[h] + jnp.dot((v_t * ws).astype(BF16), kb, preferred_element_type=F32))
        n_new.append(s0 * n[h] + _dot_pieces(jnp.broadcast_to(ws, (SUBLANES, lp)), k, _NN)[0:1, :])
        m_new.append(m_h)
    return outs, (ct_new, n_new, m_new)


def _mlstm_kernel(qk_ref, v_ref, og_ref, gt_ref, gb_ref, ng_ref, c0_ref, n0_ref, m0_ref,
                  h_ref, ct_ref, nt_ref, mt_ref, c_scr, n_scr, m_scr, *, lb, nsub, bb):
    lp, hd, nh = MLSTM_CHUNK, M_HEAD_DIM, M_HEADS

    @pl.when(pl.program_id(1) == 0)
    def _():
        for si in range(bb):
            for h in range(nh):
                c_scr[si, h] = c0_ref[si, h].T
        n_scr[...] = n0_ref[...]
        m_scr[...] = m0_ref[...]

    def pad(x):
        if lb == lp:
            return x
        return jnp.concatenate([x, jnp.zeros((lp - lb, x.shape[1]), x.dtype)], axis=0)

    for si in range(bb):
        n_all, m_all = n_scr[si], m_scr[si]
        state = ([c_scr[si, h] for h in range(nh)], [n_all[h:h + 1, :] for h in range(nh)],
                 [m_all[h:h + 1, 0:1] for h in range(nh)])
        for ci in range(nsub):
            rows = slice(ci * lb, (ci + 1) * lb)
            outs, state = _mlstm_chunk(pad(qk_ref[si, rows, :]), pad(v_ref[si, rows, :]),
                                       pad(gt_ref[si, rows, :]) + gb_ref[...], ng_ref, state, lb)
            for h in range(nh):
                cols = slice(h * hd, (h + 1) * hd)
                h_ref[si, rows, cols] = _sigmoid(og_ref[si, rows, cols]) * outs[h].T[:lb]
        ct, n, m = state
        for h in range(nh):
            c_scr[si, h] = ct[h]
        n_scr[si, 0:nh, :] = jnp.concatenate(n, axis=0)
        m_scr[si, 0:nh, :] = jnp.concatenate([jnp.broadcast_to(x, (1, LANES)) for x in m], axis=0)

    @pl.when(pl.program_id(1) == pl.num_programs(1) - 1)
    def _():
        for si in range(bb):
            for h in range(nh):
                ct_ref[si, h] = c_scr[si, h].T
        nt_ref[...] = n_scr[...]
        mt_ref[...] = m_scr[...]


def _mlstm(qk, v, og, gt, gbias, ng, c0, n0, m0, name):
    b, t, _ = qk.shape
    lb = min(t, MLSTM_CHUNK)
    nsub = min(t // lb, MLSTM_SUBCHUNKS)
    bb = 1 if nsub > 1 else min(b, MLSTM_SUBCHUNKS)
    assert t % (lb * nsub) == 0 and b % bb == 0
    tok = lambda w: pl.BlockSpec((bb, lb * nsub, w), lambda a, c: (a, c, 0))
    const = lambda shp: pl.BlockSpec(shp, lambda a, c: tuple(0 for _ in shp))
    st4 = pl.BlockSpec((bb, M_HEADS, M_HEAD_DIM, M_HEAD_DIM), lambda a, c: (a, 0, 0, 0))
    st3 = pl.BlockSpec((bb, SUBLANES, LANES), lambda a, c: (a, 0, 0))
    return pl.pallas_call(
        functools.partial(_mlstm_kernel, lb=lb, nsub=nsub, bb=bb),
        grid=(b // bb, t // (lb * nsub)),
        in_specs=[tok(2 * M_WIDTH), tok(M_WIDTH), tok(M_WIDTH), tok(LANES),
                  const((1, LANES)), const((M_HEADS, M_HEAD_DIM, MLSTM_CHUNK)), st4, st3, st3],
        out_specs=[tok(M_WIDTH), st4, st3, st3],
        out_shape=[jax.ShapeDtypeStruct((b, t, M_WIDTH), F32),
                   jax.ShapeDtypeStruct((b, M_HEADS, M_HEAD_DIM, M_HEAD_DIM), F32),
                   jax.ShapeDtypeStruct((b, SUBLANES, LANES), F32),
                   jax.ShapeDtypeStruct((b, SUBLANES, LANES), F32)],
        scratch_shapes=[pltpu.VMEM((bb, M_HEADS, M_HEAD_DIM, M_HEAD_DIM), F32),
                        pltpu.VMEM((bb, SUBLANES, LANES), F32),
                        pltpu.VMEM((bb, SUBLANES, LANES), F32)],
        compiler_params=_cparams("parallel", "arbitrary"),
        name=name,
    )(qk, v, og, gt, gbias, ng, c0, n0, m0)


def _rprep_kernel(z_ref, st_ref, mix_ref, w2_ref, a2_ref, g2_ref, w0_ref, a0_ref, kk_ref, ka_ref,
                  r_o, w_o, k_o, v_o, a_o, b_o, g_o, tail_scr, *, nb, rt):
    @pl.when(pl.program_id(1) == 0)
    def _():
        tail_scr[...] = st_ref[0]

    z = z_ref[0]
    prev = _shift_rows(z, tail_scr[...], nb)
    tail_scr[...] = z[z.shape[0] - rt:]
    xm = z + (prev - z) * mix_ref[...]
    w = R_WIDTH
    r, kr, vr = xm[:, 0:w], xm[:, w:2 * w], xm[:, 2 * w:3 * w]
    lora_in = xm[:, 3 * w:3 * w + LANES]
    gd = xm[:, 3 * w + LANES:3 * w + 2 * LANES]
    lw = _bdot(jnp.tanh(lora_in), w2_ref[...])
    la = _bdot(lora_in, a2_ref[...])
    wlog = -_softplus(-(w0_ref[...] + lw)) - 0.5
    a = _sigmoid(a0_ref[...] + la)
    ones = _pair_ones()
    kk = kr * kk_ref[...]
    kk = kk / jnp.maximum(jnp.sqrt(_head_sum64(kk * kk, ones)), 1e-12)
    r_o[0] = r
    w_o[0] = jnp.exp(-jnp.exp(wlog))
    k_o[0] = kr * (1.0 + (a - 1.0) * ka_ref[...])
    v_o[0] = vr
    a_o[0] = -kk
    b_o[0] = kk * a
    g_o[0] = _bdot(_sigmoid(gd), g2_ref[...]).astype(g_o.dtype)


def _rprep(z, state, mix, w2p, a2p, g2, w0, a0, kkw, kaw, nb, tr, name):
    g, r, c = z.shape
    rt = state.shape[1]
    assert r % tr == 0 and tr >= rt
    w = R_WIDTH
    const = lambda shp: pl.BlockSpec(shp, lambda a, i: tuple(0 for _ in shp))
    tok = pl.BlockSpec((1, tr, w), lambda a, i: (a, i, 0))
    return pl.pallas_call(
        functools.partial(_rprep_kernel, nb=nb, rt=rt),
        grid=(g, r // tr),
        in_specs=[pl.BlockSpec((1, tr, c), lambda a, i: (a, i, 0)),
                  pl.BlockSpec((1, rt, c), lambda a, i: (a, 0, 0)),
                  const((1, c)), const((LANES, w)), const((LANES, w)), const((LANES, w)),
                  const((1, w)), const((1, w)), const((1, w)), const((1, w))],
        out_specs=[tok] * 7,
        out_shape=[jax.ShapeDtypeStruct((g, r, w), F32)] * 6 + [jax.ShapeDtypeStruct((g, r, w), BF16)],
        scratch_shapes=[pltpu.VMEM((rt, c), F32)],
        compiler_params=_cparams("parallel", "arbitrary"),
        name=name,
    )(z, state, mix, w2p, a2p, g2, w0, a0, kkw, kaw)


def _rscan_kernel(r_ref, w_ref, k_ref, v_ref, a_ref, b_ref, s0_ref, y_ref, st_ref, s_scr, y_scr, *, bg, steps):
    hp = R_HEADS // 2
    npair = bg * hp
    nquad = npair // 2
    hd = R_HEAD_DIM
    tc = pl.program_id(1)
    pairs = [(bi, p) for bi in range(bg) for p in range(hp)]
    sl = lambda q: slice(q * hd, (q + 1) * hd)

    @pl.when(tc == 0)
    def _():
        for q, (bi, p) in enumerate(pairs):
            s_scr[sl(q), :] = jnp.concatenate([s0_ref[bi, 2 * p], s0_ref[bi, 2 * p + 1]], axis=1)

    y_scr[...] = jnp.zeros(y_scr.shape, F32)
    ones = _head_ones(2 * LANES)
    lane = lax.broadcasted_iota(jnp.int32, (hd, LANES), 1) & (hd - 1)
    diag = lane == lax.broadcasted_iota(jnp.int32, (hd, LANES), 0)
    low = lax.broadcasted_iota(jnp.int32, (hd, LANES), 1) < hd

    def side_by_side(xs):
        return jnp.concatenate([jnp.concatenate([xs[2 * u], xs[2 * u + 1]], axis=1) for u in range(nquad)], axis=0)

    def pair_of(res, q):
        return res[(q // 2) * hd:(q // 2 + 1) * hd, (q % 2) * LANES:(q % 2 + 1) * LANES]

    def step8(t8, carry):
        tiles = [[ref[bi, t8, :, p * LANES:(p + 1) * LANES] for bi, p in pairs]
                 for ref in (r_ref, w_ref, k_ref, v_ref, a_ref, b_ref)]
        s_cur = [s_scr[sl(q), :] for q in range(npair)]
        for j in range(SUBLANES):
            r_t, w_t, k_t, v_t, a_t, b_t = [[tl[q][j:j + 1, :] for q in range(npair)] for tl in tiles]
            prod = side_by_side([s_cur[q] * a_t[q] for q in range(npair)])
            sa = _split_dot(prod, ones, SCAN_SA_PIECES)
            vdiag = side_by_side([jnp.where(diag, v_t[q], 0.0) for q in range(npair)])
            vcol = jnp.dot(vdiag.astype(BF16), ones, preferred_element_type=F32)
            s_cur = [s_cur[q] * w_t[q] + pair_of(sa, q) * b_t[q] + pair_of(vcol, q) * k_t[q] for q in range(npair)]
            outs = side_by_side([s_cur[q] * r_t[q] for q in range(npair)])
            yb = jnp.dot(outs.astype(BF16), ones, preferred_element_type=F32)
            hit = lane == t8 * SUBLANES + j
            for q in range(npair):
                pltpu.store(y_scr.at[sl(q), :], pair_of(yb, q), mask=hit)
        for q in range(npair):
            s_scr[sl(q), :] = s_cur[q]
        return carry

    lax.fori_loop(0, steps // SUBLANES, step8, 0)

    for u in range(nquad):
        yt = y_scr[u * LANES:(u + 1) * LANES, :].T
        top, bot = yt[:hd], yt[hd:]
        bi, p0 = pairs[2 * u]
        y_ref[bi, :, p0 * LANES:(p0 + 1) * LANES] = jnp.where(low, top, pltpu.roll(bot, hd, 1))[:steps]
        y_ref[bi, :, (p0 + 1) * LANES:(p0 + 2) * LANES] = jnp.where(low, pltpu.roll(top, hd, 1), bot)[:steps]

    @pl.when(tc == pl.num_programs(1) - 1)
    def _():
        for q, (bi, p) in enumerate(pairs):
            s_fin = s_scr[sl(q), :]
            st_ref[bi, 2 * p] = s_fin[:, :hd]
            st_ref[bi, 2 * p + 1] = s_fin[:, hd:]


def _rscan(ops, s0, name):
    nbatch, nsteps, _ = ops[0].shape
    bg = SCAN_BATCH_GROUP
    steps = min(nsteps, SCAN_CHUNK)
    assert nbatch % bg == 0 and nsteps % steps == 0 and steps % SUBLANES == 0
    rows = bg * (R_HEADS // 2) * R_HEAD_DIM
    ops = [o.reshape(nbatch, nsteps // SUBLANES, SUBLANES, R_WIDTH) for o in ops]
    opspec = pl.BlockSpec((bg, steps // SUBLANES, SUBLANES, R_WIDTH), lambda g, c: (g, c, 0, 0))
    stspec = pl.BlockSpec((bg, R_HEADS, R_HEAD_DIM, R_HEAD_DIM), lambda g, c: (g, 0, 0, 0))
    return pl.pallas_call(
        functools.partial(_rscan_kernel, bg=bg, steps=steps),
        grid=(nbatch // bg, nsteps // steps),
        in_specs=[opspec] * 6 + [stspec],
        out_specs=[pl.BlockSpec((bg, steps, R_WIDTH), lambda g, c: (g, c, 0)), stspec],
        out_shape=[jax.ShapeDtypeStruct((nbatch, nsteps, R_WIDTH), F32),
                   jax.ShapeDtypeStruct(s0.shape, F32)],
        scratch_shapes=[pltpu.VMEM((rows, LANES), F32), pltpu.VMEM((rows, LANES), F32)],
        compiler_params=_cparams("parallel", "arbitrary"),
        name=name,
    )(*ops, s0)


def _merge_kernel(x_ref, hm_ref, y_ref, r_ref, k_ref, v_ref, g_ref, zg_ref, rk_ref, lng_ref, lnb_ref,
                  wbm_ref, wbr_ref, wo_ref, o_ref):
    ones = _pair_ones()
    y = y_ref[...]
    inv = 1.0 / R_HEAD_DIM
    mu = _head_sum64(y, ones) * inv
    d = y - mu
    var = _head_sum64(d * d, ones) * inv
    yn = d * lax.rsqrt(var + GN_EPS) * lng_ref[...] + lnb_ref[...]
    v = v_ref[...]
    bonus = _head_sum64(r_ref[...] * k_ref[...] * rk_ref[...], ones) * v
    yr = (yn + bonus) * g_ref[...].astype(F32)
    y_b = _bdot(yr, wbr_ref[...])
    y_a = _bdot(hm_ref[...], wbm_ref[...])
    dm = x_ref.shape[1]
    merged = _sigmoid(zg_ref[:, 0:dm].astype(F32)) * y_a + _sigmoid(zg_ref[:, dm:2 * dm].astype(F32)) * y_b
    o_ref[...] = x_ref[...] + _bdot(merged, wo_ref[...])


def _merge(x, hm, y, r, k, v, g, zg, rk, lng, lnb, wbm, wbr, wo, tm, name):
    n, d = x.shape
    w = R_WIDTH
    tok = lambda c: pl.BlockSpec((tm, c), lambda i: (i, 0))
    const = lambda shp: pl.BlockSpec(shp, lambda i: tuple(0 for _ in shp))
    return pl.pallas_call(
        _merge_kernel,
        grid=(n // tm,),
        in_specs=[tok(d), tok(M_WIDTH), tok(w), tok(w), tok(w), tok(w), tok(w), tok(2 * d),
                  const((1, w)), const((1, w)), const((1, w)),
                  const((M_WIDTH, d)), const((w, d)), const((d, d))],
        out_specs=tok(d),
        out_shape=jax.ShapeDtypeStruct((n, d), F32),
        compiler_params=_cparams("parallel"),
        name=name,
    )(x, hm, y, r, k, v, g, zg, rk, lng, lnb, wbm, wbr, wo)


def _ffn_kernel(x_ref, pe_ref, st_ref, n2g_ref, wup_ref, cw_ref, cb_ref, fd_ref, png_ref, pgw_ref,
                ppj_ref, fng_ref, y_ref, tail_ref, tail_scr, *, nb, rt, dff):
    @pl.when(pl.program_id(1) == 0)
    def _():
        tail_scr[...] = st_ref[0]

    x1 = x_ref[0]
    h2 = _rmsnorm(x1, n2g_ref[...]).astype(BF16)
    uc = jnp.dot(h2, wup_ref[:, 0:dff], preferred_element_type=F32)
    tail = tail_scr[...]
    c = cb_ref[...] + cw_ref[F_CONV - 1:F_CONV, :] * uc
    for j in range(F_CONV - 1):
        c = c + cw_ref[j:j + 1, :] * _shift_rows(uc, tail, (F_CONV - 1 - j) * nb)
    tail_scr[...] = uc[uc.shape[0] - rt:]
    ug = jnp.dot(h2, wup_ref[:, dff:2 * dff], preferred_element_type=F32)
    x2 = x1 + _bdot(_gelu_tanh(c) * ug, fd_ref[...])
    gate = _sigmoid(_bdot(_rmsnorm(x2, png_ref[...]), pgw_ref[...]))
    x3 = x2 + gate * _bdot(pe_ref[0], ppj_ref[...])
    y_ref[0] = _rmsnorm(x3, fng_ref[...])

    @pl.when(pl.program_id(1) == pl.num_programs(1) - 1)
    def _():
        tail_ref[0] = tail_scr[...]


def _ffn(x, pe, state, n2g, wup, cw8, cb, fd, png, pgw, ppj, fng, nb, tr, name):
    g, r, d = x.shape
    dff = fd.shape[0]
    pdim = pe.shape[2]
    rt = state.shape[1]
    assert r % tr == 0 and tr >= rt
    tok = lambda c: pl.BlockSpec((1, tr, c), lambda a, i: (a, i, 0))
    const = lambda shp: pl.BlockSpec(shp, lambda a, i: tuple(0 for _ in shp), pipeline_mode=pl.Buffered(1))
    tailspec = pl.BlockSpec((1, rt, dff), lambda a, i: (a, 0, 0))
    return pl.pallas_call(
        functools.partial(_ffn_kernel, nb=nb, rt=rt, dff=dff),
        grid=(g, r // tr),
        in_specs=[tok(d), tok(pdim), tailspec, const((1, d)), const((d, 2 * dff)),
                  const((SUBLANES, dff)), const((1, dff)), const((dff, d)), const((1, d)), const((d, d)),
                  const((pdim, d)), const((1, d))],
        out_specs=[tok(d), tailspec],
        out_shape=[jax.ShapeDtypeStruct((g, r, d), F32), jax.ShapeDtypeStruct((g, rt, dff), F32)],
        scratch_shapes=[pltpu.VMEM((rt, dff), F32)],
        compiler_params=_cparams("parallel", "arbitrary"),
        name=name,
    )(x, pe, state, n2g, wup, cw8, cb, fd, png, pgw, ppj, fng)


def _pad_rows(w, rows):
    return jnp.pad(w, ((0, rows - w.shape[0]), (0, 0)))


def _layer(x, pe, st, wts, nb):
    g, r, d = x.shape
    n = g * r
    nseq = g * nb
    nsteps = r // nb
    tm = 256
    xf = x.reshape(n, d)

    zqk, zv, zo, zr, zg, zif = _norm_proj(xf, wts["norm1_g"], wts["w_in"], wts["in_widths"], wts["in_dtypes"],
                                          tm, "in_proj")

    tr_m = max(tm, st["mconv"].shape[1])
    if r % tr_m:
        tr_m = r
    qk = _mconv(zqk.reshape(g, r, -1), st["mconv"], wts["m_conv_w"], wts["m_conv_b"], wts["m_qk_scale"],
                nb, tr_m, "mconv")
    to_seq = (lambda a: a.reshape(nseq, nsteps, a.shape[-1])) if nb == 1 else \
        (lambda a: a.reshape(nsteps, nseq, a.shape[-1]).transpose(1, 0, 2))
    hm, mc, mn, mm = _mlstm(to_seq(qk), to_seq(zv), to_seq(zo), to_seq(zif), wts["m_gate_bias"],
                            wts["m_norm_g"], st["mC"], st["mn"], st["mm"], "mlstm")
    hm = hm.reshape(n, -1) if nb == 1 else hm.transpose(1, 0, 2).reshape(n, -1)

    tr_r = max(tm, st["rshift"].shape[1])
    ops = _rprep(zr.reshape(g, r, -1), st["rshift"], wts["r_mix"], wts["r_w2"], wts["r_a2"], wts["r_g2"],
                 wts["r_w0"], wts["r_a0"], wts["r_kk"], wts["r_ka"], nb, tr_r, "rprep")
    rr, rw, rk, rv, ra, rb, rg = ops
    flat = lambda a: a.reshape(n, -1)
    y, s_new = _rscan([to_seq(flat(o)) for o in (rr, rw, rk, rv, ra, rb)], st["rS"], "rscan")
    y = y.reshape(n, -1) if nb == 1 else y.transpose(1, 0, 2).reshape(n, -1)

    x1 = _merge(xf, hm, y, flat(rr), flat(rk), flat(rv), flat(rg), zg, wts["r_rk"], wts["r_ln_g"],
                wts["r_ln_b"], wts["w_branch_m"], wts["w_branch_r"], wts["w_out"], tm, "merge")

    tr_f = max(FFN_ROWS, st["fconv"].shape[1])
    yout, utail = _ffn(x1.reshape(g, r, d), pe, st["fconv"], wts["norm2_g"], wts["f_up"], wts["f_conv_w"],
                       wts["f_conv_b"], wts["f_down"], wts["ple_norm_g"], wts["ple_gate_w"], wts["ple_proj"],
                       wts["final_norm_g"], nb, tr_f, "ffn")
    return yout, dict(zqk=zqk, zr=zr, utail=utail, mC=mc, mn=mn, mm=mm, rS=s_new)


def kernel(x_prompt, x_sample, state_mlstm_conv, state_mlstm_C, state_mlstm_n, state_mlstm_m,
           state_rwkv_shift, state_rwkv_S, state_ffn_conv, p_prompt, p_sample,
           norm1_g, w_in, m_conv_w, m_conv_b, m_i_bias, m_f_bias, m_norm_g, w_branch_m,
           r_mix, r_w0, r_w2, r_a0, r_a2, r_g2, r_kk, r_ka, r_rk, r_ln_g, r_ln_b, w_branch_r,
           w_out, norm2_g, f_up, f_conv_w, f_conv_b, f_down, ple_norm_g, ple_gate_w, ple_proj,
           final_norm_g):
    assert norm1_g.shape[0] == 1, "single trunk layer"
    bp, tp, d = x_prompt.shape
    bs, ts, _ = x_sample.shape
    dff = f_down.shape[1]
    mw2 = 2 * M_WIDTH
    row = lambda a: a.reshape(1, -1).astype(F32)

    wi = w_in[0]
    c_if = 4 * M_WIDTH
    c_r = c_if + 2 * M_HEADS
    c_g = c_r + R_COLS
    w_in_cat = jnp.concatenate(
        [wi[:, :c_if], wi[:, c_r:c_g], wi[:, c_g:], jnp.pad(wi[:, c_if:c_r], ((0, 0), (0, LANES - 2 * M_HEADS)))],
        axis=1).astype(BF16)
    lora = jnp.zeros((LANES, R_WIDTH), F32)
    wts = dict(
        norm1_g=row(norm1_g[0]), w_in=w_in_cat,
        in_widths=(mw2, M_WIDTH, M_WIDTH, R_COLS, 2 * d, LANES), in_dtypes=(F32, F32, F32, F32, BF16, F32),
        m_conv_w=_pad_rows(m_conv_w[0], SUBLANES), m_conv_b=row(m_conv_b[0]),
        m_qk_scale=jnp.concatenate([jnp.ones((1, M_WIDTH), F32),
                                    jnp.full((1, M_WIDTH), M_HEAD_DIM ** -0.5, F32)], axis=1),
        m_gate_bias=jnp.pad(jnp.concatenate([m_i_bias[0], m_f_bias[0]]).reshape(1, -1),
                            ((0, 0), (0, LANES - 2 * M_HEADS))),
        m_norm_g=jnp.broadcast_to(m_norm_g[0].reshape(M_HEADS, M_HEAD_DIM, 1), (M_HEADS, M_HEAD_DIM, MLSTM_CHUNK)),
        w_branch_m=w_branch_m[0].astype(BF16),
        r_mix=row(r_mix[0]), r_w0=row(r_w0[0]), r_a0=row(r_a0[0]),
        r_w2=lora.at[:R_DECAY_LORA].set(r_w2[0]).astype(BF16),
        r_a2=lora.at[R_DECAY_LORA:R_DECAY_LORA + R_A_LORA].set(r_a2[0]).astype(BF16),
        r_g2=r_g2[0].astype(BF16), r_kk=row(r_kk[0]), r_ka=row(r_ka[0]), r_rk=row(r_rk[0]),
        r_ln_g=row(r_ln_g[0]), r_ln_b=row(r_ln_b[0]), w_branch_r=w_branch_r[0].astype(BF16),
        w_out=w_out[0].astype(BF16), norm2_g=row(norm2_g[0]), f_up=f_up[0].astype(BF16),
        f_conv_w=_pad_rows(f_conv_w[0], SUBLANES), f_conv_b=row(f_conv_b[0]),
        f_down=f_down[0].astype(BF16), ple_norm_g=row(ple_norm_g[0]), ple_gate_w=ple_gate_w[0].astype(BF16),
        ple_proj=ple_proj[0].astype(BF16), final_norm_g=row(final_norm_g),
    )

    def seq_states(c, nv, m):
        b = c.shape[0]
        nv8 = jnp.pad(nv, ((0, 0), (0, SUBLANES - M_HEADS), (0, 0)))
        m8 = jnp.pad(jnp.broadcast_to(m[:, :, None], (b, M_HEADS, LANES)), ((0, 0), (0, SUBLANES - M_HEADS), (0, 0)))
        return c, nv8, m8

    zc, zn, zm = seq_states(jnp.zeros((bp, M_HEADS, M_HEAD_DIM, M_HEAD_DIM), F32),
                            jnp.zeros((bp, M_HEADS, M_HEAD_DIM), F32), jnp.zeros((bp, M_HEADS), F32))
    st_p = dict(mconv=jnp.zeros((bp, SUBLANES, mw2), F32), rshift=jnp.zeros((bp, SUBLANES, R_COLS), F32),
                fconv=jnp.zeros((bp, SUBLANES, dff), F32), mC=zc, mn=zn, mm=zm,
                rS=jnp.zeros((bp, R_HEADS, R_HEAD_DIM, R_HEAD_DIM), F32))
    y_p, new_p = _layer(x_prompt, p_prompt[0], st_p, wts, 1)

    tmaj = lambda a: a.transpose(1, 0, 2).reshape(1, a.shape[0] * a.shape[1], a.shape[2])
    sc, sn, sm = seq_states(state_mlstm_C[0], state_mlstm_n[0], state_mlstm_m[0])
    st_s = dict(mconv=tmaj(state_mlstm_conv[0]), rshift=state_rwkv_shift[0][None], fconv=tmaj(state_ffn_conv[0]),
                mC=sc, mn=sn, mm=sm, rS=state_rwkv_S[0])
    y_s, new_s = _layer(tmaj(x_sample), tmaj(p_sample[0]), st_s, wts, bs)
    y_sample = y_s.reshape(ts, bs, d).transpose(1, 0, 2)

    def finish(new, b, t, tmajor):
        if tmajor:
            last = lambda a, k: a.reshape(t, b, -1)[t - k:].transpose(1, 0, 2)
        else:
            last = lambda a, k: a.reshape(b, t, -1)[:, t - k:]
        ut = new["utail"]
        if tmajor:
            fconv = ut.reshape(-1, b, ut.shape[-1])[-(F_CONV - 1):].transpose(1, 0, 2)
        else:
            fconv = ut[:, ut.shape[1] - (F_CONV - 1):]
        return (last(new["zqk"], M_CONV - 1)[None], new["mC"][None], new["mn"][:, :M_HEADS][None],
                new["mm"][:, :M_HEADS, 0][None], last(new["zr"], 1)[:, 0][None],
                new["rS"][None], fconv[None])

    return (y_p, y_sample) + finish(new_p, bp, tp, False) + finish(new_s, bs, ts, True)
```

```python
import functools

import jax
import jax.numpy as jnp
from jax import lax
from jax.experimental import pallas as pl
from jax.experimental.pallas import tpu as pltpu

F32 = jnp.float32
BF16 = jnp.bfloat16

M_HEADS = 4
M_HEAD_DIM = 128
M_WIDTH = M_HEADS * M_HEAD_DIM
M_CONV = 4
R_HEADS = 8
R_HEAD_DIM = 64
R_WIDTH = R_HEADS * R_HEAD_DIM
R_DECAY_LORA = 64
R_A_LORA = 64
R_GATE_LORA = 128
R_COLS = 3 * R_WIDTH + R_DECAY_LORA + R_A_LORA + R_GATE_LORA
F_CONV = 3
EPS = 1e-6
GN_EPS = 64e-5

LANES = 128
SUBLANES = 8
VMEM_LIMIT_BYTES = 56 * 1024 * 1024

FFN_ROWS = 512
MLSTM_CHUNK = 128
MLSTM_SUBCHUNKS = 8
SCAN_CHUNK = 64
SCAN_BATCH_GROUP = 8
SCAN_SA_PIECES = 1


def _cparams(*sem):
    return pltpu.CompilerParams(dimension_semantics=sem, vmem_limit_bytes=VMEM_LIMIT_BYTES)


def _sigmoid(x):
    return 1.0 / (1.0 + jnp.exp(-x))


def _softplus(x):
    return jnp.maximum(x, 0.0) + jnp.log1p(jnp.exp(-jnp.abs(x)))


def _log_sigmoid(x):
    return -_softplus(-x)


def _gelu_tanh(x):
    return 0.5 * x * (1.0 + jnp.tanh(0.7978845608028654 * (x + 0.044715 * (x * x * x))))


def _bdot(a, b):
    return jnp.dot(a.astype(BF16), b.astype(BF16), preferred_element_type=F32)


def _rmsnorm(x, g):
    return x * lax.rsqrt(jnp.mean(x * x, axis=-1, keepdims=True) + EPS) * g


def _split_dot(x, m01, parts):
    acc = None
    r = x
    for p in range(parts):
        h = r.astype(BF16)
        d = jnp.dot(h, m01, preferred_element_type=F32)
        acc = d if acc is None else acc + d
        if p + 1 < parts:
            r = r - h.astype(F32)
    return acc


def _split_dot_lhs01(m01, x, parts):
    acc = None
    r = x
    for p in range(parts):
        h = r.astype(BF16)
        d = jnp.dot(m01, h, preferred_element_type=F32)
        acc = d if acc is None else acc + d
        if p + 1 < parts:
            r = r - h.astype(F32)
    return acc


_NN = (((1,), (0,)), ((), ()))
_NT = (((1,), (1,)), ((), ()))


def _dot_pieces(a, b, dims):
    a_hi, b_hi = a.astype(BF16), b.astype(BF16)
    a_lo, b_lo = (a - a_hi.astype(F32)).astype(BF16), (b - b_hi.astype(F32)).astype(BF16)
    dot = lambda x, y: lax.dot_general(x, y, dims, preferred_element_type=F32)
    return dot(a_hi, b_hi) + dot(a_hi, b_lo) + dot(a_lo, b_hi)


def _head_ones(n):
    r = lax.broadcasted_iota(jnp.int32, (n, n), 0) // R_HEAD_DIM
    c = lax.broadcasted_iota(jnp.int32, (n, n), 1) // R_HEAD_DIM
    return jnp.where(r == c, 1.0, 0.0).astype(BF16)


def _pair_ones():
    return _head_ones(LANES)


def _head_sum64(x, ones):
    outs = [_split_dot(x[:, s:s + LANES], ones, 3) for s in range(0, x.shape[1], LANES)]
    return outs[0] if len(outs) == 1 else jnp.concatenate(outs, axis=1)


def _shift_rows(x, tail, s):
    n, rt = x.shape[0], tail.shape[0]
    if s % SUBLANES == 0:
        if s == n:
            return tail[rt - s:]
        return jnp.concatenate([tail[rt - s:], x[:n - s]], axis=0)
    assert s < SUBLANES
    r = pltpu.roll(x, s, 0)
    t8 = pltpu.roll(tail[rt - SUBLANES:], s, 0)
    row = lax.broadcasted_iota(jnp.int32, (SUBLANES, x.shape[1]), 0)
    first = jnp.where(row < s, t8, r[:SUBLANES])
    if n == SUBLANES:
        return first
    return jnp.concatenate([first, r[SUBLANES:]], axis=0)


def _front_kernel(x_ref, mst_ref, rst_ref, n1g_ref, w_ref, cw_ref, cb_ref, qsc_ref, mix_ref, w2_ref, a2_ref,
                  g2_ref, w0_ref, a0_ref, kk_ref, ka_ref, rk_ref,
                  qk_o, zv_o, zo_o, zg_o, zif_o, r_o, w_o, k_o, v_o, a_o, b_o, g_o, bonus_o, mtail_o, rtail_o,
                  mtail_scr, rtail_scr, *, nb, widths):
    @pl.when(pl.program_id(1) == 0)
    def _():
        mtail_scr[...] = mst_ref[0]
        rtail_scr[...] = rst_ref[0]

    h = _rmsnorm(x_ref[0], n1g_ref[...]).astype(BF16)
    offs = [sum(widths[:i]) for i in range(len(widths))]
    proj = lambda i: jnp.dot(h, w_ref[:, offs[i]:offs[i] + widths[i]], preferred_element_type=F32)
    tr = h.shape[0]

    zqk = proj(0)
    mtail = mtail_scr[...]
    y = cb_ref[...] + cw_ref[M_CONV - 1:M_CONV, :] * zqk
    for j in range(M_CONV - 1):
        y = y + cw_ref[j:j + 1, :] * _shift_rows(zqk, mtail, (M_CONV - 1 - j) * nb)
    mtail_scr[...] = zqk[tr - mtail.shape[0]:]
    qk_o[0] = y * _sigmoid(y) * qsc_ref[...]
    zv_o[0] = proj(1)
    zo_o[0] = proj(2)
    zg_o[0] = proj(4).astype(zg_o.dtype)
    zif_o[0] = proj(5)

    z = proj(3)
    rtail = rtail_scr[...]
    prev = _shift_rows(z, rtail, nb)
    rtail_scr[...] = z[tr - rtail.shape[0]:]
    xm = z + (prev - z) * mix_ref[...]
    w = R_WIDTH
    r, kr, vr = xm[:, 0:w], xm[:, w:2 * w], xm[:, 2 * w:3 * w]
    lora_in = xm[:, 3 * w:3 * w + LANES]
    gd = xm[:, 3 * w + LANES:3 * w + 2 * LANES]
    lw = _bdot(jnp.tanh(lora_in), w2_ref[...])
    la = _bdot(lora_in, a2_ref[...])
    wlog = -_softplus(-(w0_ref[...] + lw)) - 0.5
    a = _sigmoid(a0_ref[...] + la)
    ones = _pair_ones()
    kk = kr * kk_ref[...]
    kk = kk / jnp.maximum(jnp.sqrt(_head_sum64(kk * kk, ones)), 1e-12)
    k2 = kr * (1.0 + (a - 1.0) * ka_ref[...])
    r_o[0] = r
    w_o[0] = jnp.exp(-jnp.exp(wlog))
    k_o[0] = k2
    v_o[0] = vr
    a_o[0] = -kk
    b_o[0] = kk * a
    g_o[0] = _bdot(_sigmoid(gd), g2_ref[...]).astype(g_o.dtype)
    bonus_o[0] = (_head_sum64(r * k2 * rk_ref[...], ones) * vr).astype(bonus_o.dtype)

    @pl.when(pl.program_id(1) == pl.num_programs(1) - 1)
    def _():
        mtail_o[0] = mtail_scr[...]
        rtail_o[0] = rtail_scr[...]


def _front(x, mstate, rstate, wts, nb, tr, name):
    g, r, d = x.shape
    widths = wts["in_widths"]
    rt_m, rt_r = mstate.shape[1], rstate.shape[1]
    assert r % tr == 0 and tr >= max(rt_m, rt_r)
    mw2, w = 2 * M_WIDTH, R_WIDTH
    tok = lambda c: pl.BlockSpec((1, tr, c), lambda a, i: (a, i, 0))
    const = lambda shp: pl.BlockSpec(shp, lambda a, i: tuple(0 for _ in shp), pipeline_mode=pl.Buffered(1))
    tail = lambda rows, c: pl.BlockSpec((1, rows, c), lambda a, i: (a, 0, 0))
    out_cols = [(mw2, F32), (M_WIDTH, F32), (M_WIDTH, F32), (2 * d, BF16), (LANES, F32)] \
        + [(w, F32)] * 6 + [(w, BF16), (w, BF16)]
    return pl.pallas_call(
        functools.partial(_front_kernel, nb=nb, widths=tuple(widths)),
        grid=(g, r // tr),
        in_specs=[tok(d), tail(rt_m, mw2), tail(rt_r, R_COLS), const((1, d)), const((d, sum(widths))),
                  const((SUBLANES, mw2)), const((1, mw2)), const((1, mw2)), const((1, R_COLS)),
                  const((LANES, w)), const((LANES, w)), const((LANES, w)),
                  const((1, w)), const((1, w)), const((1, w)), const((1, w)), const((1, w))],
        out_specs=[tok(c) for c, _ in out_cols] + [tail(rt_m, mw2), tail(rt_r, R_COLS)],
        out_shape=[jax.ShapeDtypeStruct((g, r, c), dt) for c, dt in out_cols]
        + [jax.ShapeDtypeStruct((g, rt_m, mw2), F32), jax.ShapeDtypeStruct((g, rt_r, R_COLS), F32)],
        scratch_shapes=[pltpu.VMEM((rt_m, mw2), F32), pltpu.VMEM((rt_r, R_COLS), F32)],
        compiler_params=_cparams("parallel", "arbitrary"),
        name=name,
    )(x, mstate, rstate, wts["norm1_g"], wts["w_in"], wts["m_conv_w"], wts["m_conv_b"], wts["m_qk_scale"],
      wts["r_mix"], wts["r_w2"], wts["r_a2"], wts["r_g2"], wts["r_w0"], wts["r_a0"], wts["r_kk"], wts["r_ka"],
      wts["r_rk"])


def _mlstm_chunk(qk, v, gates, ng_ref, state, lb):
    lp, hd, nh = MLSTM_CHUNK, M_HEAD_DIM, M_HEADS
    row = lax.broadcasted_iota(jnp.int32, (lp, LANES), 0)
    lane = lax.broadcasted_iota(jnp.int32, (lp, LANES), 1)
    valid = row < lb
    li = jnp.where(valid, gates, -jnp.inf)
    lf = jnp.where(valid & (lane >= nh) & (lane < 2 * nh), _log_sigmoid(gates), 0.0)
    sq_r = lax.broadcasted_iota(jnp.int32, (lp, lp), 0)
    sq_c = lax.broadcasted_iota(jnp.int32, (lp, lp), 1)
    tril = jnp.where(sq_r >= sq_c, 1.0, 0.0).astype(BF16)
    bcum = _split_dot_lhs01(tril, lf, 3)
    xg = jnp.where(lane < nh, li, bcum)
    xg_t = xg.T

    keep = sq_r <= sq_c
    ct, n, m = state
    outs, ct_new, n_new, m_new = [], [], [], []
    for h in range(nh):
        brow = xg_t[nh + h:nh + h + 1, :]
        irow = xg_t[h:h + 1, :]
        ccol = xg[:, h:h + 1] - xg[:, nh + h:nh + h + 1]
        q = qk[:, h * hd:(h + 1) * hd]
        k = qk[:, M_WIDTH + h * hd:M_WIDTH + (h + 1) * hd]
        qb, kb = q.astype(BF16), k.astype(BF16)
        v_t = v[:, h * hd:(h + 1) * hd].T

        dm = jnp.where(keep, brow + ccol, -jnp.inf)
        inter = brow + m[h]
        mt = jnp.maximum(inter, jnp.max(dm, axis=0, keepdims=True))
        s_t = lax.dot_general(kb, qb, _NT, preferred_element_type=F32)
        p_t = jnp.exp(dm - mt) * s_t
        sc = jnp.exp(inter - mt)
        num = sc * lax.dot_general(ct[h].astype(BF16), qb, _NT, preferred_element_type=F32) \
            + jnp.dot(v_t.astype(BF16), p_t.astype(BF16), preferred_element_type=F32)
        qn = _dot_pieces(jnp.broadcast_to(n[h], (SUBLANES, hd)), q, _NT)[0:1, :]
        den = sc * qn + jnp.sum(p_t, axis=0, keepdims=True)
        hh = num / jnp.maximum(jnp.abs(den), jnp.exp(-mt))
        mu = jnp.mean(hh, axis=0, keepdims=True)
        d = hh - mu
        outs.append(d * lax.rsqrt(jnp.mean(d * d, axis=0, keepdims=True) + EPS) * ng_ref[h])

        btot = brow[:, lp - 1:lp]
        wlog = btot - brow + irow
        m_h = jnp.maximum(btot + m[h], jnp.max(wlog, axis=1, keepdims=True))
        s0 = jnp.exp(btot + m[h] - m_h)
        ws = jnp.exp(wlog - m_h)
        ct_new.append(s0 * ct[h] + jnp.dot((v_t * ws).astype(BF16), kb, preferred_element_type=F32))
        n_new.append(s0 * n[h] + _dot_pieces(jnp.broadcast_to(ws, (SUBLANES, lp)), k, _NN)[0:1, :])
        m_new.append(m_h)
    return outs, (ct_new, n_new, m_new)


def _mlstm_kernel(qk_ref, v_ref, og_ref, gt_ref, gb_ref, ng_ref, c0_ref, n0_ref, m0_ref,
                  h_ref, ct_ref, nt_ref, mt_ref, c_scr, n_scr, m_scr, *, lb, nsub, bb):
    lp, hd, nh = MLSTM_CHUNK, M_HEAD_DIM, M_HEADS

    @pl.when(pl.program_id(1) == 0)
    def _():
        for si in range(bb):
            for h in range(nh):
                c_scr[si, h] = c0_ref[si, h].T
        n_scr[...] = n0_ref[...]
        m_scr[...] = m0_ref[...]

    def pad(x):
        if lb == lp:
            return x
        return jnp.concatenate([x, jnp.zeros((lp - lb, x.shape[1]), x.dtype)], axis=0)

    for si in range(bb):
        n_all, m_all = n_scr[si], m_scr[si]
        state = ([c_scr[si, h] for h in range(nh)], [n_all[h:h + 1, :] for h in range(nh)],
                 [m_all[h:h + 1, 0:1] for h in range(nh)])
        for ci in range(nsub):
            rows = slice(ci * lb, (ci + 1) * lb)
            outs, state = _mlstm_chunk(pad(qk_ref[si, rows, :]), pad(v_ref[si, rows, :]),
                                       pad(gt_ref[si, rows, :]) + gb_ref[...], ng_ref, state, lb)
            for h in range(nh):
                cols = slice(h * hd, (h + 1) * hd)
                h_ref[si, rows, cols] = _sigmoid(og_ref[si, rows, cols]) * outs[h].T[:lb]
        ct, n, m = state
        for h in range(nh):
            c_scr[si, h] = ct[h]
        n_scr[si, 0:nh, :] = jnp.concatenate(n, axis=0)
        m_scr[si, 0:nh, :] = jnp.concatenate([jnp.broadcast_to(x, (1, LANES)) for x in m], axis=0)

    @pl.when(pl.program_id(1) == pl.num_programs(1) - 1)
    def _():
        for si in range(bb):
            for h in range(nh):
                ct_ref[si, h] = c_scr[si, h].T
        nt_ref[...] = n_scr[...]
        mt_ref[...] = m_scr[...]


def _mlstm(qk, v, og, gt, gbias, ng, c0, n0, m0, name):
    b, t, _ = qk.shape
    lb = min(t, MLSTM_CHUNK)
    nsub = min(t // lb, MLSTM_SUBCHUNKS)
    bb = 1 if nsub > 1 else min(b, MLSTM_SUBCHUNKS)
    assert t % (lb * nsub) == 0 and b % bb == 0
    tok = lambda w: pl.BlockSpec((bb, lb * nsub, w), lambda a, c: (a, c, 0))
    const = lambda shp: pl.BlockSpec(shp, lambda a, c: tuple(0 for _ in shp))
    st4 = pl.BlockSpec((bb, M_HEADS, M_HEAD_DIM, M_HEAD_DIM), lambda a, c: (a, 0, 0, 0))
    st3 = pl.BlockSpec((bb, SUBLANES, LANES), lambda a, c: (a, 0, 0))
    return pl.pallas_call(
        functools.partial(_mlstm_kernel, lb=lb, nsub=nsub, bb=bb),
        grid=(b // bb, t // (lb * nsub)),
        in_specs=[tok(2 * M_WIDTH), tok(M_WIDTH), tok(M_WIDTH), tok(LANES),
                  const((1, LANES)), const((M_HEADS, M_HEAD_DIM, MLSTM_CHUNK)), st4, st3, st3],
        out_specs=[tok(M_WIDTH), st4, st3, st3],
        out_shape=[jax.ShapeDtypeStruct((b, t, M_WIDTH), F32),
                   jax.ShapeDtypeStruct((b, M_HEADS, M_HEAD_DIM, M_HEAD_DIM), F32),
                   jax.ShapeDtypeStruct((b, SUBLANES, LANES), F32),
                   jax.ShapeDtypeStruct((b, SUBLANES, LANES), F32)],
        scratch_shapes=[pltpu.VMEM((bb, M_HEADS, M_HEAD_DIM, M_HEAD_DIM), F32),
                        pltpu.VMEM((bb, SUBLANES, LANES), F32),
                        pltpu.VMEM((bb, SUBLANES, LANES), F32)],
        compiler_params=_cparams("parallel", "arbitrary"),
        name=name,
    )(qk, v, og, gt, gbias, ng, c0, n0, m0)


def _rscan_kernel(r_ref, w_ref, k_ref, v_ref, a_ref, b_ref, s0_ref, y_ref, st_ref, s_scr, y_scr, *, bg, steps):
    hp = R_HEADS // 2
    npair = bg * hp
    nquad = npair // 2
    hd = R_HEAD_DIM
    tc = pl.program_id(1)
    pairs = [(bi, p) for bi in range(bg) for p in range(hp)]
    sl = lambda q: slice(q * hd, (q + 1) * hd)

    @pl.when(tc == 0)
    def _():
        for q, (bi, p) in enumerate(pairs):
            s_scr[sl(q), :] = jnp.concatenate([s0_ref[bi, 2 * p], s0_ref[bi, 2 * p + 1]], axis=1)

    y_scr[...] = jnp.zeros(y_scr.shape, F32)
    ones = _head_ones(2 * LANES)
    lane = lax.broadcasted_iota(jnp.int32, (hd, LANES), 1) & (hd - 1)
    diag = lane == lax.broadcasted_iota(jnp.int32, (hd, LANES), 0)
    low = lax.broadcasted_iota(jnp.int32, (hd, LANES), 1) < hd

    def side_by_side(xs):
        return jnp.concatenate([jnp.concatenate([xs[2 * u], xs[2 * u + 1]], axis=1) for u in range(nquad)], axis=0)

    def pair_of(res, q):
        return res[(q // 2) * hd:(q // 2 + 1) * hd, (q % 2) * LANES:(q % 2 + 1) * LANES]

    def step8(t8, carry):
        tiles = [[ref[bi, t8, :, p * LANES:(p + 1) * LANES] for bi, p in pairs]
                 for ref in (r_ref, w_ref, k_ref, v_ref, a_ref, b_ref)]
        s_cur = [s_scr[sl(q), :] for q in range(npair)]
        for j in range(SUBLANES):
            r_t, w_t, k_t, v_t, a_t, b_t = [[tl[q][j:j + 1, :] for q in range(npair)] for tl in tiles]
            prod = side_by_side([s_cur[q] * a_t[q] for q in range(npair)])
            sa = _split_dot(prod, ones, SCAN_SA_PIECES)
            vdiag = side_by_side([jnp.where(diag, v_t[q], 0.0) for q in range(npair)])
            vcol = jnp.dot(vdiag.astype(BF16), ones, preferred_element_type=F32)
            s_cur = [s_cur[q] * w_t[q] + pair_of(sa, q) * b_t[q] + pair_of(vcol, q) * k_t[q] for q in range(npair)]
            outs = side_by_side([s_cur[q] * r_t[q] for q in range(npair)])
            yb = jnp.dot(outs.astype(BF16), ones, preferred_element_type=F32)
            hit = lane == t8 * SUBLANES + j
            for q in range(npair):
                pltpu.store(y_scr.at[sl(q), :], pair_of(yb, q), mask=hit)
        for q in range(npair):
            s_scr[sl(q), :] = s_cur[q]
        return carry

    lax.fori_loop(0, steps // SUBLANES, step8, 0)

    for u in range(nquad):
        yt = y_scr[u * LANES:(u + 1) * LANES, :].T
        top, bot = yt[:hd], yt[hd:]
        bi, p0 = pairs[2 * u]
        y_ref[bi, :, p0 * LANES:(p0 + 1) * LANES] = jnp.where(low, top, pltpu.roll(bot, hd, 1))[:steps]
        y_ref[bi, :, (p0 + 1) * LANES:(p0 + 2) * LANES] = jnp.where(low, pltpu.roll(top, hd, 1), bot)[:steps]

    @pl.when(tc == pl.num_programs(1) - 1)
    def _():
        for q, (bi, p) in enumerate(pairs):
            s_fin = s_scr[sl(q), :]
            st_ref[bi, 2 * p] = s_fin[:, :hd]
            st_ref[bi, 2 * p + 1] = s_fin[:, hd:]


def _rscan(ops, s0, name):
    nbatch, nsteps, _ = ops[0].shape
    bg = SCAN_BATCH_GROUP
    steps = min(nsteps, SCAN_CHUNK)
    assert nbatch % bg == 0 and nsteps % steps == 0 and steps % SUBLANES == 0
    rows = bg * (R_HEADS // 2) * R_HEAD_DIM
    ops = [o.reshape(nbatch, nsteps // SUBLANES, SUBLANES, R_WIDTH) for o in ops]
    opspec = pl.BlockSpec((bg, steps // SUBLANES, SUBLANES, R_WIDTH), lambda g, c: (g, c, 0, 0))
    stspec = pl.BlockSpec((bg, R_HEADS, R_HEAD_DIM, R_HEAD_DIM), lambda g, c: (g, 0, 0, 0))
    return pl.pallas_call(
        functools.partial(_rscan_kernel, bg=bg, steps=steps),
        grid=(nbatch // bg, nsteps // steps),
        in_specs=[opspec] * 6 + [stspec],
        out_specs=[pl.BlockSpec((bg, steps, R_WIDTH), lambda g, c: (g, c, 0)), stspec],
        out_shape=[jax.ShapeDtypeStruct((nbatch, nsteps, R_WIDTH), F32),
                   jax.ShapeDtypeStruct(s0.shape, F32)],
        scratch_shapes=[pltpu.VMEM((rows, LANES), F32), pltpu.VMEM((rows, LANES), F32)],
        compiler_params=_cparams("parallel", "arbitrary"),
        name=name,
    )(*ops, s0)


def _merge_kernel(x_ref, hm_ref, y_ref, bonus_ref, g_ref, zg_ref, lng_ref, lnb_ref,
                  wbm_ref, wbr_ref, wo_ref, o_ref):
    ones = _pair_ones()
    y = y_ref[...]
    inv = 1.0 / R_HEAD_DIM
    mu = _head_sum64(y, ones) * inv
    d = y - mu
    var = _head_sum64(d * d, ones) * inv
    yn = d * lax.rsqrt(var + GN_EPS) * lng_ref[...] + lnb_ref[...]
    yr = (yn + bonus_ref[...].astype(F32)) * g_ref[...].astype(F32)
    y_b = _bdot(yr, wbr_ref[...])
    y_a = _bdot(hm_ref[...], wbm_ref[...])
    dm = x_ref.shape[1]
    merged = _sigmoid(zg_ref[:, 0:dm].astype(F32)) * y_a + _sigmoid(zg_ref[:, dm:2 * dm].astype(F32)) * y_b
    o_ref[...] = x_ref[...] + _bdot(merged, wo_ref[...])


def _merge(x, hm, y, bonus, g, zg, lng, lnb, wbm, wbr, wo, tm, name):
    n, d = x.shape
    w = R_WIDTH
    tok = lambda c: pl.BlockSpec((tm, c), lambda i: (i, 0))
    const = lambda shp: pl.BlockSpec(shp, lambda i: tuple(0 for _ in shp))
    return pl.pallas_call(
        _merge_kernel,
        grid=(n // tm,),
        in_specs=[tok(d), tok(M_WIDTH), tok(w), tok(w), tok(w), tok(2 * d),
                  const((1, w)), const((1, w)),
                  const((M_WIDTH, d)), const((w, d)), const((d, d))],
        out_specs=tok(d),
        out_shape=jax.ShapeDtypeStruct((n, d), F32),
        compiler_params=_cparams("parallel"),
        name=name,
    )(x, hm, y, bonus, g, zg, lng, lnb, wbm, wbr, wo)


def _ffn_kernel(x_ref, pe_ref, st_ref, n2g_ref, wup_ref, cw_ref, cb_ref, fd_ref, png_ref, pgw_ref,
                ppj_ref, fng_ref, y_ref, tail_ref, tail_scr, *, nb, rt, dff):
    @pl.when(pl.program_id(1) == 0)
    def _():
        tail_scr[...] = st_ref[0]

    x1 = x_ref[0]
    h2 = _rmsnorm(x1, n2g_ref[...]).astype(BF16)
    uc = jnp.dot(h2, wup_ref[:, 0:dff], preferred_element_type=F32)
    tail = tail_scr[...]
    c = cb_ref[...] + cw_ref[F_CONV - 1:F_CONV, :] * uc
    for j in range(F_CONV - 1):
        c = c + cw_ref[j:j + 1, :] * _shift_rows(uc, tail, (F_CONV - 1 - j) * nb)
    tail_scr[...] = uc[uc.shape[0] - rt:]
    ug = jnp.dot(h2, wup_ref[:, dff:2 * dff], preferred_element_type=F32)
    x2 = x1 + _bdot(_gelu_tanh(c) * ug, fd_ref[...])
    gate = _sigmoid(_bdot(_rmsnorm(x2, png_ref[...]), pgw_ref[...]))
    x3 = x2 + gate * _bdot(pe_ref[0], ppj_ref[...])
    y_ref[0] = _rmsnorm(x3, fng_ref[...])

    @pl.when(pl.program_id(1) == pl.num_programs(1) - 1)
    def _():
        tail_ref[0] = tail_scr[...]


def _ffn(x, pe, state, n2g, wup, cw8, cb, fd, png, pgw, ppj, fng, nb, tr, name):
    g, r, d = x.shape
    dff = fd.shape[0]
    pdim = pe.shape[2]
    rt = state.shape[1]
    assert r % tr == 0 and tr >= rt
    tok = lambda c: pl.BlockSpec((1, tr, c), lambda a, i: (a, i, 0))
    const = lambda shp: pl.BlockSpec(shp, lambda a, i: tuple(0 for _ in shp), pipeline_mode=pl.Buffered(1))
    tailspec = pl.BlockSpec((1, rt, dff), lambda a, i: (a, 0, 0))
    return pl.pallas_call(
        functools.partial(_ffn_kernel, nb=nb, rt=rt, dff=dff),
        grid=(g, r // tr),
        in_specs=[tok(d), tok(pdim), tailspec, const((1, d)), const((d, 2 * dff)),
                  const((SUBLANES, dff)), const((1, dff)), const((dff, d)), const((1, d)), const((d, d)),
                  const((pdim, d)), const((1, d))],
        out_specs=[tok(d), tailspec],
        out_shape=[jax.ShapeDtypeStruct((g, r, d), F32), jax.ShapeDtypeStruct((g, rt, dff), F32)],
        scratch_shapes=[pltpu.VMEM((rt, dff), F32)],
        compiler_params=_cparams("parallel", "arbitrary"),
        name=name,
    )(x, pe, state, n2g, wup, cw8, cb, fd, png, pgw, ppj, fng)


def _pad_rows(w, rows):
    return jnp.pad(w, ((0, rows - w.shape[0]), (0, 0)))


def _layer(x, pe, st, wts, nb):
    g, r, d = x.shape
    n = g * r
    nseq = g * nb
    nsteps = r // nb
    tm = 256
    xf = x.reshape(n, d)
    flat = lambda a: a.reshape(n, -1)

    tr = max(tm, st["mconv"].shape[1], st["rshift"].shape[1])
    tr = tr if r % tr == 0 else 2 * tm
    (qk, zv, zo, zg, zif, rr, rw, rk, rv, ra, rb, rg, bonus, mtail, rtail) = _front(
        x, st["mconv"], st["rshift"], wts, nb, tr, "front")

    to_seq = (lambda a: a.reshape(nseq, nsteps, a.shape[-1])) if nb == 1 else \
        (lambda a: a.reshape(nsteps, nseq, a.shape[-1]).transpose(1, 0, 2))
    from_seq = (lambda a: a.reshape(n, -1)) if nb == 1 else (lambda a: a.transpose(1, 0, 2).reshape(n, -1))

    hm, mc, mn, mm = _mlstm(to_seq(flat(qk)), to_seq(flat(zv)), to_seq(flat(zo)), to_seq(flat(zif)),
                            wts["m_gate_bias"], wts["m_norm_g"], st["mC"], st["mn"], st["mm"], "mlstm")

    y, s_new = _rscan([to_seq(flat(o)) for o in (rr, rw, rk, rv, ra, rb)], st["rS"], "rscan")

    x1 = _merge(xf, from_seq(hm), from_seq(y), flat(bonus), flat(rg), flat(zg), wts["r_ln_g"], wts["r_ln_b"],
                wts["w_branch_m"], wts["w_branch_r"], wts["w_out"], tm, "merge")

    tr_f = max(FFN_ROWS, st["fconv"].shape[1])
    yout, utail = _ffn(x1.reshape(g, r, d), pe, st["fconv"], wts["norm2_g"], wts["f_up"], wts["f_conv_w"],
                       wts["f_conv_b"], wts["f_down"], wts["ple_norm_g"], wts["ple_gate_w"], wts["ple_proj"],
                       wts["final_norm_g"], nb, tr_f, "ffn")
    return yout, dict(mtail=mtail, rtail=rtail, utail=utail, mC=mc, mn=mn, mm=mm, rS=s_new)


def kernel(x_prompt, x_sample, state_mlstm_conv, state_mlstm_C, state_mlstm_n, state_mlstm_m,
           state_rwkv_shift, state_rwkv_S, state_ffn_conv, p_prompt, p_sample,
           norm1_g, w_in, m_conv_w, m_conv_b, m_i_bias, m_f_bias, m_norm_g, w_branch_m,
           r_mix, r_w0, r_w2, r_a0, r_a2, r_g2, r_kk, r_ka, r_rk, r_ln_g, r_ln_b, w_branch_r,
           w_out, norm2_g, f_up, f_conv_w, f_conv_b, f_down, ple_norm_g, ple_gate_w, ple_proj,
           final_norm_g):
    assert norm1_g.shape[0] == 1, "single trunk layer"
    bp, tp, d = x_prompt.shape
    bs, ts, _ = x_sample.shape
    dff = f_down.shape[1]
    mw2 = 2 * M_WIDTH
    row = lambda a: a.reshape(1, -1).astype(F32)

    wi = w_in[0]
    c_if = 4 * M_WIDTH
    c_r = c_if + 2 * M_HEADS
    c_g = c_r + R_COLS
    w_in_cat = jnp.concatenate(
        [wi[:, :c_if], wi[:, c_r:c_g], wi[:, c_g:], jnp.pad(wi[:, c_if:c_r], ((0, 0), (0, LANES - 2 * M_HEADS)))],
        axis=1).astype(BF16)
    lora = jnp.zeros((LANES, R_WIDTH), F32)
    wts = dict(
        norm1_g=row(norm1_g[0]), w_in=w_in_cat,
        in_widths=(mw2, M_WIDTH, M_WIDTH, R_COLS, 2 * d, LANES),
        m_conv_w=_pad_rows(m_conv_w[0], SUBLANES), m_conv_b=row(m_conv_b[0]),
        m_qk_scale=jnp.concatenate([jnp.ones((1, M_WIDTH), F32),
                                    jnp.full((1, M_WIDTH), M_HEAD_DIM ** -0.5, F32)], axis=1),
        m_gate_bias=jnp.pad(jnp.concatenate([m_i_bias[0], m_f_bias[0]]).reshape(1, -1),
                            ((0, 0), (0, LANES - 2 * M_HEADS))),
        m_norm_g=jnp.broadcast_to(m_norm_g[0].reshape(M_HEADS, M_HEAD_DIM, 1), (M_HEADS, M_HEAD_DIM, MLSTM_CHUNK)),
        w_branch_m=w_branch_m[0].astype(BF16),
        r_mix=row(r_mix[0]), r_w0=row(r_w0[0]), r_a0=row(r_a0[0]),
        r_w2=lora.at[:R_DECAY_LORA].set(r_w2[0]).astype(BF16),
        r_a2=lora.at[R_DECAY_LORA:R_DECAY_LORA + R_A_LORA].set(r_a2[0]).astype(BF16),
        r_g2=r_g2[0].astype(BF16), r_kk=row(r_kk[0]), r_ka=row(r_ka[0]), r_rk=row(r_rk[0]),
        r_ln_g=row(r_ln_g[0]), r_ln_b=row(r_ln_b[0]), w_branch_r=w_branch_r[0].astype(BF16),
        w_out=w_out[0].astype(BF16), norm2_g=row(norm2_g[0]), f_up=f_up[0].astype(BF16),
        f_conv_w=_pad_rows(f_conv_w[0], SUBLANES), f_conv_b=row(f_conv_b[0]),
        f_down=f_down[0].astype(BF16), ple_norm_g=row(ple_norm_g[0]), ple_gate_w=ple_gate_w[0].astype(BF16),
        ple_proj=ple_proj[0].astype(BF16), final_norm_g=row(final_norm_g),
    )

    def seq_states(c, nv, m):
        b = c.shape[0]
        nv8 = jnp.pad(nv, ((0, 0), (0, SUBLANES - M_HEADS), (0, 0)))
        m8 = jnp.pad(jnp.broadcast_to(m[:, :, None], (b, M_HEADS, LANES)), ((0, 0), (0, SUBLANES - M_HEADS), (0, 0)))
        return c, nv8, m8

    zc, zn, zm = seq_states(jnp.zeros((bp, M_HEADS, M_HEAD_DIM, M_HEAD_DIM), F32),
                            jnp.zeros((bp, M_HEADS, M_HEAD_DIM), F32), jnp.zeros((bp, M_HEADS), F32))
    st_p = dict(mconv=jnp.zeros((bp, SUBLANES, mw2), F32), rshift=jnp.zeros((bp, SUBLANES, R_COLS), F32),
                fconv=jnp.zeros((bp, SUBLANES, dff), F32), mC=zc, mn=zn, mm=zm,
                rS=jnp.zeros((bp, R_HEADS, R_HEAD_DIM, R_HEAD_DIM), F32))
    y_p, new_p = _layer(x_prompt, p_prompt[0], st_p, wts, 1)

    tmaj = lambda a: a.transpose(1, 0, 2).reshape(1, a.shape[0] * a.shape[1], a.shape[2])
    sc, sn, sm = seq_states(state_mlstm_C[0], state_mlstm_n[0], state_mlstm_m[0])
    st_s = dict(mconv=tmaj(state_mlstm_conv[0]), rshift=state_rwkv_shift[0][None], fconv=tmaj(state_ffn_conv[0]),
                mC=sc, mn=sn, mm=sm, rS=state_rwkv_S[0])
    y_s, new_s = _layer(tmaj(x_sample), tmaj(p_sample[0]), st_s, wts, bs)
    y_sample = y_s.reshape(ts, bs, d).transpose(1, 0, 2)

    def finish(new, b, tmajor):
        def last(tail, k):
            if tmajor:
                return tail.reshape(-1, b, tail.shape[-1])[-k:].transpose(1, 0, 2)
            return tail[:, tail.shape[1] - k:]
        return (last(new["mtail"], M_CONV - 1)[None], new["mC"][None], new["mn"][:, :M_HEADS][None],
                new["mm"][:, :M_HEADS, 0][None], last(new["rtail"], 1)[:, 0][None],
                new["rS"][None], last(new["utail"], F_CONV - 1)[None])

    return (y_p, y_sample) + finish(new_p, bp, False) + finish(new_s, bs, True)
```

```python
import functools

import jax
import jax.numpy as jnp
from jax import lax
from jax.experimental import pallas as pl
from jax.experimental.pallas import tpu as pltpu

F32 = jnp.float32
BF16 = jnp.bfloat16

M_HEADS = 4
M_HEAD_DIM = 128
M_WIDTH = M_HEADS * M_HEAD_DIM
M_CONV = 4
R_HEADS = 8
R_HEAD_DIM = 64
R_WIDTH = R_HEADS * R_HEAD_DIM
R_DECAY_LORA = 64
R_A_LORA = 64
R_GATE_LORA = 128
R_COLS = 3 * R_WIDTH + R_DECAY_LORA + R_A_LORA + R_GATE_LORA
F_CONV = 3
EPS = 1e-6
GN_EPS = 64e-5

LANES = 128
SUBLANES = 8
VMEM_LIMIT_BYTES = 56 * 1024 * 1024

FRONT_ROWS = 512
MERGE_ROWS = 512
FFN_ROWS = 512
MLSTM_CHUNK = 128
MLSTM_SUBCHUNKS = 8
SCAN_CHUNK = 64
SCAN_BATCH_GROUP = 8
SCAN_SA_PIECES = 1


def _cparams(*sem):
    return pltpu.CompilerParams(dimension_semantics=sem, vmem_limit_bytes=VMEM_LIMIT_BYTES)


def _sigmoid(x):
    return 1.0 / (1.0 + jnp.exp(-x))


def _softplus(x):
    return jnp.maximum(x, 0.0) + jnp.log1p(jnp.exp(-jnp.abs(x)))


def _log_sigmoid(x):
    return -_softplus(-x)


def _gelu_tanh(x):
    return 0.5 * x * (1.0 + jnp.tanh(0.7978845608028654 * (x + 0.044715 * (x * x * x))))


def _bdot(a, b):
    return jnp.dot(a.astype(BF16), b.astype(BF16), preferred_element_type=F32)


def _rmsnorm(x, g):
    return x * lax.rsqrt(jnp.mean(x * x, axis=-1, keepdims=True) + EPS) * g


def _split_dot(x, m01, parts):
    acc = None
    r = x
    for p in range(parts):
        h = r.astype(BF16)
        d = jnp.dot(h, m01, preferred_element_type=F32)
        acc = d if acc is None else acc + d
        if p + 1 < parts:
            r = r - h.astype(F32)
    return acc


def _split_dot_lhs01(m01, x, parts):
    acc = None
    r = x
    for p in range(parts):
        h = r.astype(BF16)
        d = jnp.dot(m01, h, preferred_element_type=F32)
        acc = d if acc is None else acc + d
        if p + 1 < parts:
            r = r - h.astype(F32)
    return acc


_NN = (((1,), (0,)), ((), ()))
_NT = (((1,), (1,)), ((), ()))


def _dot_pieces(a, b, dims):
    a_hi, b_hi = a.astype(BF16), b.astype(BF16)
    a_lo, b_lo = (a - a_hi.astype(F32)).astype(BF16), (b - b_hi.astype(F32)).astype(BF16)
    dot = lambda x, y: lax.dot_general(x, y, dims, preferred_element_type=F32)
    return dot(a_hi, b_hi) + dot(a_hi, b_lo) + dot(a_lo, b_hi)


def _head_ones(n):
    r = lax.broadcasted_iota(jnp.int32, (n, n), 0) // R_HEAD_DIM
    c = lax.broadcasted_iota(jnp.int32, (n, n), 1) // R_HEAD_DIM
    return jnp.where(r == c, 1.0, 0.0).astype(BF16)


def _head_sum64(x, ones):
    wd = ones.shape[0]
    outs = [_split_dot(x[:, s:s + wd], ones, 2) for s in range(0, x.shape[1], wd)]
    return outs[0] if len(outs) == 1 else jnp.concatenate(outs, axis=1)


def _shift_rows(x, tail, s):
    n, rt = x.shape[0], tail.shape[0]
    if s % SUBLANES == 0:
        if s == n:
            return tail[rt - s:]
        return jnp.concatenate([tail[rt - s:], x[:n - s]], axis=0)
    assert s < SUBLANES
    r = pltpu.roll(x, s, 0)
    t8 = pltpu.roll(tail[rt - SUBLANES:], s, 0)
    row = lax.broadcasted_iota(jnp.int32, (SUBLANES, x.shape[1]), 0)
    first = jnp.where(row < s, t8, r[:SUBLANES])
    if n == SUBLANES:
        return first
    return jnp.concatenate([first, r[SUBLANES:]], axis=0)


def _front_kernel(x_ref, mst_ref, rst_ref, n1g_ref, wm_ref, wr_ref, wg_ref, wif_ref, cw_ref, cb_ref, qsc_ref, mix_ref, w2_ref, a2_ref,
                  g2_ref, w0_ref, a0_ref, kk_ref, ka_ref, rk_ref,
                  qk_o, zv_o, zo_o, zg_o, zif_o, r_o, w_o, k_o, v_o, a_o, b_o, g_o, bonus_o, mtail_o, rtail_o,
                  mtail_scr, rtail_scr, *, nb):
    @pl.when(pl.program_id(1) == 0)
    def _():
        mtail_scr[...] = mst_ref[0]
        rtail_scr[...] = rst_ref[0]

    h = _rmsnorm(x_ref[0], n1g_ref[...]).astype(BF16)
    proj = lambda w: jnp.dot(h, w, preferred_element_type=F32)
    tr = h.shape[0]
    mw2 = 2 * M_WIDTH

    z = proj(wr_ref[...])
    rtail = rtail_scr[...]
    prev = _shift_rows(z, rtail, nb)
    rtail_scr[...] = z[tr - rtail.shape[0]:]
    xm = z + (prev - z) * mix_ref[...]
    w = R_WIDTH
    r, kr, vr = xm[:, 0:w], xm[:, w:2 * w], xm[:, 2 * w:3 * w]
    lora_in = xm[:, 3 * w:3 * w + LANES]
    gd = xm[:, 3 * w + LANES:3 * w + 2 * LANES]
    lw = _bdot(jnp.tanh(lora_in), w2_ref[...])
    la = _bdot(lora_in, a2_ref[...])
    wlog = -_softplus(-(w0_ref[...] + lw)) - 0.5
    a = _sigmoid(a0_ref[...] + la)
    ones = _head_ones(2 * LANES)
    kk = kr * kk_ref[...]
    kk = kk / jnp.maximum(jnp.sqrt(_head_sum64(kk * kk, ones)), 1e-12)
    k2 = kr * (1.0 + (a - 1.0) * ka_ref[...])
    r_o[0] = r
    w_o[0] = jnp.exp(-jnp.exp(wlog))
    k_o[0] = k2
    v_o[0] = vr
    a_o[0] = -kk
    b_o[0] = kk * a
    g_o[0] = _bdot(_sigmoid(gd), g2_ref[...]).astype(g_o.dtype)
    bonus_o[0] = (_head_sum64(r * k2 * rk_ref[...], ones) * vr).astype(bonus_o.dtype)

    zqk = proj(wm_ref[:, 0:mw2])
    mtail = mtail_scr[...]
    y = cb_ref[...] + cw_ref[M_CONV - 1:M_CONV, :] * zqk
    for j in range(M_CONV - 1):
        y = y + cw_ref[j:j + 1, :] * _shift_rows(zqk, mtail, (M_CONV - 1 - j) * nb)
    mtail_scr[...] = zqk[tr - mtail.shape[0]:]
    qk_o[0] = y * _sigmoid(y) * qsc_ref[...]
    zv_o[0] = proj(wm_ref[:, mw2:mw2 + M_WIDTH])
    zo_o[0] = proj(wm_ref[:, mw2 + M_WIDTH:mw2 + 2 * M_WIDTH])
    zg_o[0] = proj(wg_ref[...]).astype(zg_o.dtype)
    zif_o[0] = proj(wif_ref[...])

    @pl.when(pl.program_id(1) == pl.num_programs(1) - 1)
    def _():
        mtail_o[0] = mtail_scr[...]
        rtail_o[0] = rtail_scr[...]


def _front(x, mstate, rstate, wts, nb, tr, name):
    g, r, d = x.shape
    rt_m, rt_r = mstate.shape[1], rstate.shape[1]
    assert r % tr == 0 and tr >= max(rt_m, rt_r)
    mw2, w = 2 * M_WIDTH, R_WIDTH
    tok = lambda c: pl.BlockSpec((1, tr, c), lambda a, i: (a, i, 0))
    const = lambda shp: pl.BlockSpec(shp, lambda a, i: tuple(0 for _ in shp), pipeline_mode=pl.Buffered(1))
    tail = lambda rows, c: pl.BlockSpec((1, rows, c), lambda a, i: (a, 0, 0))
    out_cols = [(mw2, F32), (M_WIDTH, F32), (M_WIDTH, F32), (2 * d, BF16), (LANES, F32)] \
        + [(w, F32)] * 6 + [(w, BF16), (w, BF16)]
    return pl.pallas_call(
        functools.partial(_front_kernel, nb=nb),
        grid=(g, r // tr),
        in_specs=[tok(d), tail(rt_m, mw2), tail(rt_r, R_COLS), const((1, d)),
                  const((d, 2 * mw2)), const((d, R_COLS)), const((d, 2 * d)), const((d, LANES)),
                  const((SUBLANES, mw2)), const((1, mw2)), const((1, mw2)), const((1, R_COLS)),
                  const((LANES, w)), const((LANES, w)), const((LANES, w)),
                  const((1, w)), const((1, w)), const((1, w)), const((1, w)), const((1, w))],
        out_specs=[tok(c) for c, _ in out_cols] + [tail(rt_m, mw2), tail(rt_r, R_COLS)],
        out_shape=[jax.ShapeDtypeStruct((g, r, c), dt) for c, dt in out_cols]
        + [jax.ShapeDtypeStruct((g, rt_m, mw2), F32), jax.ShapeDtypeStruct((g, rt_r, R_COLS), F32)],
        scratch_shapes=[pltpu.VMEM((rt_m, mw2), F32), pltpu.VMEM((rt_r, R_COLS), F32)],
        compiler_params=_cparams("parallel", "arbitrary"),
        name=name,
    )(x, mstate, rstate, wts["norm1_g"], wts["w_m"], wts["w_r"], wts["w_g"], wts["w_if"], wts["m_conv_w"], wts["m_conv_b"], wts["m_qk_scale"],
      wts["r_mix"], wts["r_w2"], wts["r_a2"], wts["r_g2"], wts["r_w0"], wts["r_a0"], wts["r_kk"], wts["r_ka"],
      wts["r_rk"])


def _mlstm_chunk(qk, v, gates, ng_ref, state, lb):
    lp, hd, nh = MLSTM_CHUNK, M_HEAD_DIM, M_HEADS
    row = lax.broadcasted_iota(jnp.int32, (lp, LANES), 0)
    lane = lax.broadcasted_iota(jnp.int32, (lp, LANES), 1)
    valid = row < lb
    li = jnp.where(valid, gates, -jnp.inf)
    lf = jnp.where(valid & (lane >= nh) & (lane < 2 * nh), _log_sigmoid(gates), 0.0)
    sq_r = lax.broadcasted_iota(jnp.int32, (lp, lp), 0)
    sq_c = lax.broadcasted_iota(jnp.int32, (lp, lp), 1)
    tril = jnp.where(sq_r >= sq_c, 1.0, 0.0).astype(BF16)
    bcum = _split_dot_lhs01(tril, lf, 2)
    xg = jnp.where(lane < nh, li, bcum)
    xg_t = xg.T

    keep = sq_r <= sq_c
    ct, n, m = state
    outs, ct_new, n_new, m_new = [], [], [], []
    for h in range(nh):
        brow = xg_t[nh + h:nh + h + 1, :]
        irow = xg_t[h:h + 1, :]
        ccol = xg[:, h:h + 1] - xg[:, nh + h:nh + h + 1]
        q = qk[:, h * hd:(h + 1) * hd]
        k = qk[:, M_WIDTH + h * hd:M_WIDTH + (h + 1) * hd]
        qb, kb = q.astype(BF16), k.astype(BF16)
        v_t = v[:, h * hd:(h + 1) * hd].T

        dm = jnp.where(keep, brow + ccol, -jnp.inf)
        inter = brow + m[h]
        mt = jnp.maximum(inter, jnp.max(dm, axis=0, keepdims=True))
        s_t = lax.dot_general(kb, qb, _NT, preferred_element_type=F32)
        p_t = jnp.exp(dm - mt) * s_t
        sc = jnp.exp(inter - mt)
        num = sc * lax.dot_general(ct[h].astype(BF16), qb, _NT, preferred_element_type=F32) \
            + jnp.dot(v_t.astype(BF16), p_t.astype(BF16), preferred_element_type=F32)
        qn = _dot_pieces(jnp.broadcast_to(n[h], (SUBLANES, hd)), q, _NT)[0:1, :]
        den = sc * qn + jnp.sum(p_t, axis=0, keepdims=True)
        hh = num / jnp.maximum(jnp.abs(den), jnp.exp(-mt))
        mu = jnp.mean(hh, axis=0, keepdims=True)
        d = hh - mu
        outs.append(d * lax.rsqrt(jnp.mean(d * d, axis=0, keepdims=True) + EPS) * ng_ref[h])

        btot = brow[:, lp - 1:lp]
        wlog = btot - brow + irow
        m_h = jnp.maximum(btot + m[h], jnp.max(wlog, axis=1, keepdims=True))
        s0 = jnp.exp(btot + m[h] - m_h)
        ws = jnp.exp(wlog - m_h)
        ct_new.append(s0 * ct[h] + jnp.dot((v_t * ws).astype(BF16), kb, preferred_element_type=F32))
        n_new.append(s0 * n[h] + _dot_pieces(jnp.broadcast_to(ws, (SUBLANES, lp)), k, _NN)[0:1, :])
        m_new.append(m_h)
    return outs, (ct_new, n_new, m_new)


def _mlstm_kernel(qk_ref, v_ref, og_ref, gt_ref, gb_ref, ng_ref, c0_ref, n0_ref, m0_ref,
                  h_ref, ct_ref, nt_ref, mt_ref, c_scr, n_scr, m_scr, *, lb, nsub, bb):
    lp, hd, nh = MLSTM_CHUNK, M_HEAD_DIM, M_HEADS

    @pl.when(pl.program_id(1) == 0)
    def _():
        for si in range(bb):
            for h in range(nh):
                c_scr[si, h] = c0_ref[si, h].T
        n_scr[...] = n0_ref[...]
        m_scr[...] = m0_ref[...]

    def pad(x):
        if lb == lp:
            return x
        return jnp.concatenate([x, jnp.zeros((lp - lb, x.shape[1]), x.dtype)], axis=0)

    for si in range(bb):
        n_all, m_all = n_scr[si], m_scr[si]
        state = ([c_scr[si, h] for h in range(nh)], [n_all[h:h + 1, :] for h in range(nh)],
                 [m_all[h:h + 1, 0:1] for h in range(nh)])
        for ci in range(nsub):
            rows = slice(ci * lb, (ci + 1) * lb)
            outs, state = _mlstm_chunk(pad(qk_ref[si, rows, :]), pad(v_ref[si, rows, :]),
                                       pad(gt_ref[si, rows, :]) + gb_ref[...], ng_ref, state, lb)
            for h in range(nh):
                cols = slice(h * hd, (h + 1) * hd)
                h_ref[si, rows, cols] = _sigmoid(og_ref[si, rows, cols]) * outs[h].T[:lb]
        ct, n, m = state
        for h in range(nh):
            c_scr[si, h] = ct[h]
        n_scr[si, 0:nh, :] = jnp.concatenate(n, axis=0)
        m_scr[si, 0:nh, :] = jnp.concatenate([jnp.broadcast_to(x, (1, LANES)) for x in m], axis=0)

    @pl.when(pl.program_id(1) == pl.num_programs(1) - 1)
    def _():
        for si in range(bb):
            for h in range(nh):
                ct_ref[si, h] = c_scr[si, h].T
        nt_ref[...] = n_scr[...]
        mt_ref[...] = m_scr[...]


def _mlstm(qk, v, og, gt, gbias, ng, c0, n0, m0, name):
    b, t, _ = qk.shape
    lb = min(t, MLSTM_CHUNK)
    nsub = min(t // lb, MLSTM_SUBCHUNKS)
    bb = 1 if nsub > 1 else min(b, MLSTM_SUBCHUNKS)
    assert t % (lb * nsub) == 0 and b % bb == 0
    tok = lambda w: pl.BlockSpec((bb, lb * nsub, w), lambda a, c: (a, c, 0))
    const = lambda shp: pl.BlockSpec(shp, lambda a, c: tuple(0 for _ in shp))
    st4 = pl.BlockSpec((bb, M_HEADS, M_HEAD_DIM, M_HEAD_DIM), lambda a, c: (a, 0, 0, 0))
    st3 = pl.BlockSpec((bb, SUBLANES, LANES), lambda a, c: (a, 0, 0))
    return pl.pallas_call(
        functools.partial(_mlstm_kernel, lb=lb, nsub=nsub, bb=bb),
        grid=(b // bb, t // (lb * nsub)),
        in_specs=[tok(2 * M_WIDTH), tok(M_WIDTH), tok(M_WIDTH), tok(LANES),
                  const((1, LANES)), const((M_HEADS, M_HEAD_DIM, MLSTM_CHUNK)), st4, st3, st3],
        out_specs=[tok(M_WIDTH), st4, st3, st3],
        out_shape=[jax.ShapeDtypeStruct((b, t, M_WIDTH), F32),
                   jax.ShapeDtypeStruct((b, M_HEADS, M_HEAD_DIM, M_HEAD_DIM), F32),
                   jax.ShapeDtypeStruct((b, SUBLANES, LANES), F32),
                   jax.ShapeDtypeStruct((b, SUBLANES, LANES), F32)],
        scratch_shapes=[pltpu.VMEM((bb, M_HEADS, M_HEAD_DIM, M_HEAD_DIM), F32),
                        pltpu.VMEM((bb, SUBLANES, LANES), F32),
                        pltpu.VMEM((bb, SUBLANES, LANES), F32)],
        compiler_params=_cparams("parallel", "arbitrary"),
        name=name,
    )(qk, v, og, gt, gbias, ng, c0, n0, m0)


def _rscan_kernel(r_ref, w_ref, k_ref, v_ref, a_ref, b_ref, s0_ref, y_ref, st_ref, s_scr, y_scr, *, bg, steps):
    hp = R_HEADS // 2
    npair = bg * hp
    nquad = npair // 2
    hd = R_HEAD_DIM
    tc = pl.program_id(1)
    pairs = [(bi, p) for bi in range(bg) for p in range(hp)]
    sl = lambda q: slice(q * hd, (q + 1) * hd)

    @pl.when(tc == 0)
    def _():
        for q, (bi, p) in enumerate(pairs):
            s_scr[sl(q), :] = jnp.concatenate([s0_ref[bi, 2 * p], s0_ref[bi, 2 * p + 1]], axis=1)

    y_scr[...] = jnp.zeros(y_scr.shape, F32)
    ones = _head_ones(2 * LANES)
    lane = lax.broadcasted_iota(jnp.int32, (hd, LANES), 1) & (hd - 1)
    diag = lane == lax.broadcasted_iota(jnp.int32, (hd, LANES), 0)

    def side_by_side(xs):
        return jnp.concatenate([jnp.concatenate([xs[2 * u], xs[2 * u + 1]], axis=1) for u in range(nquad)], axis=0)

    def pair_of(res, q):
        return res[(q // 2) * hd:(q // 2 + 1) * hd, (q % 2) * LANES:(q % 2 + 1) * LANES]

    def step8(t8, carry):
        tiles = [[ref[bi, t8, :, p * LANES:(p + 1) * LANES] for bi, p in pairs]
                 for ref in (r_ref, w_ref, k_ref, v_ref, a_ref, b_ref)]
        s_cur = [s_scr[sl(q), :] for q in range(npair)]
        for j in range(SUBLANES):
            r_t, w_t, k_t, v_t, a_t, b_t = [[tl[q][j:j + 1, :] for q in range(npair)] for tl in tiles]
            prod = side_by_side([s_cur[q] * a_t[q] for q in range(npair)])
            sa = _split_dot(prod, ones, SCAN_SA_PIECES)
            vdiag = side_by_side([jnp.where(diag, v_t[q], 0.0) for q in range(npair)])
            vcol = jnp.dot(vdiag.astype(BF16), ones, preferred_element_type=F32)
            s_cur = [s_cur[q] * w_t[q] + pair_of(sa, q) * b_t[q] + pair_of(vcol, q) * k_t[q] for q in range(npair)]
            outs = side_by_side([s_cur[q] * r_t[q] for q in range(npair)])
            yb = jnp.dot(outs.astype(BF16), ones, preferred_element_type=F32)
            hit = lane == t8 * SUBLANES + j
            for q in range(npair):
                pltpu.store(y_scr.at[sl(q), :], pair_of(yb, q), mask=hit)
        for q in range(npair):
            s_scr[sl(q), :] = s_cur[q]
        return carry

    lax.fori_loop(0, steps // SUBLANES, step8, 0)

    low = lax.broadcasted_iota(jnp.int32, (hd, LANES), 1) < hd
    for u in range(nquad):
        yt = y_scr[u * LANES:(u + 1) * LANES, :].T
        top, bot = yt[:hd], yt[hd:]
        bi, p0 = pairs[2 * u]
        y_ref[bi, :, p0 * LANES:(p0 + 1) * LANES] = jnp.where(low, top, pltpu.roll(bot, hd, 1))[:steps]
        y_ref[bi, :, (p0 + 1) * LANES:(p0 + 2) * LANES] = jnp.where(low, pltpu.roll(top, hd, 1), bot)[:steps]

    @pl.when(tc == pl.num_programs(1) - 1)
    def _():
        for q, (bi, p) in enumerate(pairs):
            s_fin = s_scr[sl(q), :]
            st_ref[bi, 2 * p] = s_fin[:, :hd]
            st_ref[bi, 2 * p + 1] = s_fin[:, hd:]


def _rscan(ops, s0, name):
    nbatch, nsteps, _ = ops[0].shape
    bg = SCAN_BATCH_GROUP
    steps = min(nsteps, SCAN_CHUNK)
    assert nbatch % bg == 0 and nsteps % steps == 0 and steps % SUBLANES == 0
    rows = bg * (R_HEADS // 2) * R_HEAD_DIM
    ops = [o.reshape(nbatch, nsteps // SUBLANES, SUBLANES, R_WIDTH) for o in ops]
    opspec = pl.BlockSpec((bg, steps // SUBLANES, SUBLANES, R_WIDTH), lambda g, c: (g, c, 0, 0))
    stspec = pl.BlockSpec((bg, R_HEADS, R_HEAD_DIM, R_HEAD_DIM), lambda g, c: (g, 0, 0, 0))
    return pl.pallas_call(
        functools.partial(_rscan_kernel, bg=bg, steps=steps),
        grid=(nbatch // bg, nsteps // steps),
        in_specs=[opspec] * 6 + [stspec],
        out_specs=[pl.BlockSpec((bg, steps, R_WIDTH), lambda g, c: (g, c, 0)), stspec],
        out_shape=[jax.ShapeDtypeStruct((nbatch, nsteps, R_WIDTH), F32),
                   jax.ShapeDtypeStruct(s0.shape, F32)],
        scratch_shapes=[pltpu.VMEM((rows, LANES), F32), pltpu.VMEM((rows, LANES), F32)],
        compiler_params=_cparams("parallel", "arbitrary"),
        name=name,
    )(*ops, s0)


def _merge_kernel(x_ref, hm_ref, y_ref, bonus_ref, g_ref, zg_ref, lng_ref, lnb_ref,
                  wbm_ref, wbr_ref, wo_ref, o_ref):
    ones = _head_ones(2 * LANES)
    y = y_ref[...]
    inv = 1.0 / R_HEAD_DIM
    mu = _head_sum64(y, ones) * inv
    d = y - mu
    var = _head_sum64(d * d, ones) * inv
    yn = d * lax.rsqrt(var + GN_EPS) * lng_ref[...] + lnb_ref[...]
    yr = (yn + bonus_ref[...].astype(F32)) * g_ref[...].astype(F32)
    y_b = _bdot(yr, wbr_ref[...])
    y_a = _bdot(hm_ref[...], wbm_ref[...])
    dm = x_ref.shape[1]
    merged = _sigmoid(zg_ref[:, 0:dm].astype(F32)) * y_a + _sigmoid(zg_ref[:, dm:2 * dm].astype(F32)) * y_b
    o_ref[...] = x_ref[...] + _bdot(merged, wo_ref[...])


def _merge(x, hm, y, bonus, g, zg, lng, lnb, wbm, wbr, wo, tm, name):
    n, d = x.shape
    w = R_WIDTH
    tok = lambda c: pl.BlockSpec((tm, c), lambda i: (i, 0))
    const = lambda shp: pl.BlockSpec(shp, lambda i: tuple(0 for _ in shp))
    return pl.pallas_call(
        _merge_kernel,
        grid=(n // tm,),
        in_specs=[tok(d), tok(M_WIDTH), tok(w), tok(w), tok(w), tok(2 * d),
                  const((1, w)), const((1, w)),
                  const((M_WIDTH, d)), const((w, d)), const((d, d))],
        out_specs=tok(d),
        out_shape=jax.ShapeDtypeStruct((n, d), F32),
        compiler_params=_cparams("parallel"),
        name=name,
    )(x, hm, y, bonus, g, zg, lng, lnb, wbm, wbr, wo)


def _ffn_kernel(x_ref, pe_ref, st_ref, n2g_ref, wup_ref, cw_ref, cb_ref, fd_ref, png_ref, pgw_ref,
                ppj_ref, fng_ref, y_ref, tail_ref, tail_scr, *, nb, rt, dff):
    @pl.when(pl.program_id(1) == 0)
    def _():
        tail_scr[...] = st_ref[0]

    x1 = x_ref[0]
    h2 = _rmsnorm(x1, n2g_ref[...]).astype(BF16)
    uc = jnp.dot(h2, wup_ref[:, 0:dff], preferred_element_type=F32)
    tail = tail_scr[...]
    c = cb_ref[...] + cw_ref[F_CONV - 1:F_CONV, :] * uc
    for j in range(F_CONV - 1):
        c = c + cw_ref[j:j + 1, :] * _shift_rows(uc, tail, (F_CONV - 1 - j) * nb)
    tail_scr[...] = uc[uc.shape[0] - rt:]
    ug = jnp.dot(h2, wup_ref[:, dff:2 * dff], preferred_element_type=F32)
    x2 = x1 + _bdot(_gelu_tanh(c) * ug, fd_ref[...])
    gate = _sigmoid(_bdot(_rmsnorm(x2, png_ref[...]), pgw_ref[...]))
    x3 = x2 + gate * _bdot(pe_ref[0], ppj_ref[...])
    y_ref[0] = _rmsnorm(x3, fng_ref[...])

    @pl.when(pl.program_id(1) == pl.num_programs(1) - 1)
    def _():
        tail_ref[0] = tail_scr[...]


def _ffn(x, pe, state, n2g, wup, cw8, cb, fd, png, pgw, ppj, fng, nb, tr, name):
    g, r, d = x.shape
    dff = fd.shape[0]
    pdim = pe.shape[2]
    rt = state.shape[1]
    assert r % tr == 0 and tr >= rt
    tok = lambda c: pl.BlockSpec((1, tr, c), lambda a, i: (a, i, 0))
    const = lambda shp: pl.BlockSpec(shp, lambda a, i: tuple(0 for _ in shp), pipeline_mode=pl.Buffered(1))
    tailspec = pl.BlockSpec((1, rt, dff), lambda a, i: (a, 0, 0))
    return pl.pallas_call(
        functools.partial(_ffn_kernel, nb=nb, rt=rt, dff=dff),
        grid=(g, r // tr),
        in_specs=[tok(d), tok(pdim), tailspec, const((1, d)), const((d, 2 * dff)),
                  const((SUBLANES, dff)), const((1, dff)), const((dff, d)), const((1, d)), const((d, d)),
                  const((pdim, d)), const((1, d))],
        out_specs=[tok(d), tailspec],
        out_shape=[jax.ShapeDtypeStruct((g, r, d), F32), jax.ShapeDtypeStruct((g, rt, dff), F32)],
        scratch_shapes=[pltpu.VMEM((rt, dff), F32)],
        compiler_params=_cparams("parallel", "arbitrary"),
        name=name,
    )(x, pe, state, n2g, wup, cw8, cb, fd, png, pgw, ppj, fng)


def _pad_rows(w, rows):
    return jnp.pad(w, ((0, rows - w.shape[0]), (0, 0)))


def _layer(x, pe, st, wts, nb):
    g, r, d = x.shape
    n = g * r
    nseq = g * nb
    nsteps = r // nb
    xf = x.reshape(n, d)
    flat = lambda a: a.reshape(n, -1)

    tr = FRONT_ROWS
    (qk, zv, zo, zg, zif, rr, rw, rk, rv, ra, rb, rg, bonus, mtail, rtail) = _front(
        x, st["mconv"], st["rshift"], wts, nb, tr, "front")

    to_seq = (lambda a: a.reshape(nseq, nsteps, a.shape[-1])) if nb == 1 else \
        (lambda a: a.reshape(nsteps, nseq, a.shape[-1]).transpose(1, 0, 2))
    from_seq = (lambda a: a.reshape(n, -1)) if nb == 1 else (lambda a: a.transpose(1, 0, 2).reshape(n, -1))

    hm, mc, mn, mm = _mlstm(to_seq(flat(qk)), to_seq(flat(zv)), to_seq(flat(zo)), to_seq(flat(zif)),
                            wts["m_gate_bias"], wts["m_norm_g"], st["mC"], st["mn"], st["mm"], "mlstm")

    y, s_new = _rscan([to_seq(flat(o)) for o in (rr, rw, rk, rv, ra, rb)], st["rS"], "rscan")

    x1 = _merge(xf, from_seq(hm), from_seq(y), flat(bonus), flat(rg), flat(zg), wts["r_ln_g"], wts["r_ln_b"],
                wts["w_branch_m"], wts["w_branch_r"], wts["w_out"], MERGE_ROWS, "merge")

    tr_f = max(FFN_ROWS, st["fconv"].shape[1])
    yout, utail = _ffn(x1.reshape(g, r, d), pe, st["fconv"], wts["norm2_g"], wts["f_up"], wts["f_conv_w"],
                       wts["f_conv_b"], wts["f_down"], wts["ple_norm_g"], wts["ple_gate_w"], wts["ple_proj"],
                       wts["final_norm_g"], nb, tr_f, "ffn")
    return yout, dict(mtail=mtail, rtail=rtail, utail=utail, mC=mc, mn=mn, mm=mm, rS=s_new)


def kernel(x_prompt, x_sample, state_mlstm_conv, state_mlstm_C, state_mlstm_n, state_mlstm_m,
           state_rwkv_shift, state_rwkv_S, state_ffn_conv, p_prompt, p_sample,
           norm1_g, w_in, m_conv_w, m_conv_b, m_i_bias, m_f_bias, m_norm_g, w_branch_m,
           r_mix, r_w0, r_w2, r_a0, r_a2, r_g2, r_kk, r_ka, r_rk, r_ln_g, r_ln_b, w_branch_r,
           w_out, norm2_g, f_up, f_conv_w, f_conv_b, f_down, ple_norm_g, ple_gate_w, ple_proj,
           final_norm_g):
    assert norm1_g.shape[0] == 1, "single trunk layer"
    bp, tp, d = x_prompt.shape
    bs, ts, _ = x_sample.shape
    dff = f_down.shape[1]
    mw2 = 2 * M_WIDTH
    row = lambda a: a.reshape(1, -1).astype(F32)

    wi = w_in[0]
    c_if = 4 * M_WIDTH
    c_r = c_if + 2 * M_HEADS
    c_g = c_r + R_COLS
    lora = jnp.zeros((LANES, R_WIDTH), F32)
    wts = dict(
        norm1_g=row(norm1_g[0]),
        w_m=wi[:, :c_if].astype(BF16), w_r=wi[:, c_r:c_g].astype(BF16), w_g=wi[:, c_g:].astype(BF16),
        w_if=jnp.pad(wi[:, c_if:c_r], ((0, 0), (0, LANES - 2 * M_HEADS))).astype(BF16),
        m_conv_w=_pad_rows(m_conv_w[0], SUBLANES), m_conv_b=row(m_conv_b[0]),
        m_qk_scale=jnp.concatenate([jnp.ones((1, M_WIDTH), F32),
                                    jnp.full((1, M_WIDTH), M_HEAD_DIM ** -0.5, F32)], axis=1),
        m_gate_bias=jnp.pad(jnp.concatenate([m_i_bias[0], m_f_bias[0]]).reshape(1, -1),
                            ((0, 0), (0, LANES - 2 * M_HEADS))),
        m_norm_g=jnp.broadcast_to(m_norm_g[0].reshape(M_HEADS, M_HEAD_DIM, 1), (M_HEADS, M_HEAD_DIM, MLSTM_CHUNK)),
        w_branch_m=w_branch_m[0].astype(BF16),
        r_mix=row(r_mix[0]), r_w0=row(r_w0[0]), r_a0=row(r_a0[0]),
        r_w2=lora.at[:R_DECAY_LORA].set(r_w2[0]).astype(BF16),
        r_a2=lora.at[R_DECAY_LORA:R_DECAY_LORA + R_A_LORA].set(r_a2[0]).astype(BF16),
        r_g2=r_g2[0].astype(BF16), r_kk=row(r_kk[0]), r_ka=row(r_ka[0]), r_rk=row(r_rk[0]),
        r_ln_g=row(r_ln_g[0]), r_ln_b=row(r_ln_b[0]), w_branch_r=w_branch_r[0].astype(BF16),
        w_out=w_out[0].astype(BF16), norm2_g=row(norm2_g[0]), f_up=f_up[0].astype(BF16),
        f_conv_w=_pad_rows(f_conv_w[0], SUBLANES), f_conv_b=row(f_conv_b[0]),
        f_down=f_down[0].astype(BF16), ple_norm_g=row(ple_norm_g[0]), ple_gate_w=ple_gate_w[0].astype(BF16),
        ple_proj=ple_proj[0].astype(BF16), final_norm_g=row(final_norm_g),
    )

    def seq_states(c, nv, m):
        b = c.shape[0]
        nv8 = jnp.pad(nv, ((0, 0), (0, SUBLANES - M_HEADS), (0, 0)))
        m8 = jnp.pad(jnp.broadcast_to(m[:, :, None], (b, M_HEADS, LANES)), ((0, 0), (0, SUBLANES - M_HEADS), (0, 0)))
        return c, nv8, m8

    zc, zn, zm = seq_states(jnp.zeros((bp, M_HEADS, M_HEAD_DIM, M_HEAD_DIM), F32),
                            jnp.zeros((bp, M_HEADS, M_HEAD_DIM), F32), jnp.zeros((bp, M_HEADS), F32))
    st_p = dict(mconv=jnp.zeros((bp, SUBLANES, mw2), F32), rshift=jnp.zeros((bp, SUBLANES, R_COLS), F32),
                fconv=jnp.zeros((bp, SUBLANES, dff), F32), mC=zc, mn=zn, mm=zm,
                rS=jnp.zeros((bp, R_HEADS, R_HEAD_DIM, R_HEAD_DIM), F32))
    y_p, new_p = _layer(x_prompt, p_prompt[0], st_p, wts, 1)

    tmaj = lambda a: a.transpose(1, 0, 2).reshape(1, a.shape[0] * a.shape[1], a.shape[2])
    sc, sn, sm = seq_states(state_mlstm_C[0], state_mlstm_n[0], state_mlstm_m[0])
    st_s = dict(mconv=tmaj(state_mlstm_conv[0]), rshift=state_rwkv_shift[0][None], fconv=tmaj(state_ffn_conv[0]),
                mC=sc, mn=sn, mm=sm, rS=state_rwkv_S[0])
    y_s, new_s = _layer(tmaj(x_sample), tmaj(p_sample[0]), st_s, wts, bs)
    y_sample = y_s.reshape(ts, bs, d).transpose(1, 0, 2)

    def finish(new, b, tmajor):
        def last(tail, k):
            if tmajor:
                return tail.reshape(-1, b, tail.shape[-1])[-k:].transpose(1, 0, 2)
            return tail[:, tail.shape[1] - k:]
        return (last(new["mtail"], M_CONV - 1)[None], new["mC"][None], new["mn"][:, :M_HEADS][None],
                new["mm"][:, :M_HEADS, 0][None], last(new["rtail"], 1)[:, 0][None],
                new["rS"][None], last(new["utail"], F_CONV - 1)[None])

    return (y_p, y_sample) + finish(new_p, bp, False) + finish(new_s, bs, True)
```

```python
import functools

import jax
import jax.numpy as jnp
from jax import lax
from jax.experimental import pallas as pl
from jax.experimental.pallas import tpu as pltpu

F32 = jnp.float32
BF16 = jnp.bfloat16

M_HEADS = 4
M_HEAD_DIM = 128
M_WIDTH = M_HEADS * M_HEAD_DIM
M_CONV = 4
R_HEADS = 8
R_HEAD_DIM = 64
R_WIDTH = R_HEADS * R_HEAD_DIM
R_DECAY_LORA = 64
R_A_LORA = 64
R_GATE_LORA = 128
R_COLS = 3 * R_WIDTH + R_DECAY_LORA + R_A_LORA + R_GATE_LORA
F_CONV = 3
EPS = 1e-6
GN_EPS = 64e-5

LANES = 128
SUBLANES = 8
VMEM_LIMIT_BYTES = 56 * 1024 * 1024

FRONT_ROWS = 512
MERGE_ROWS = 512
FFN_ROWS = 512
MLSTM_CHUNK = 256
MLSTM_MIN_CHUNK = 128
MLSTM_SUBCHUNKS = 8
MLSTM_SHORT_SEQS = 8
SCAN_CHUNK = 64
SCAN_BATCH_GROUP = 8
SCAN_SA_PIECES = 1


def _cparams(*sem):
    return pltpu.CompilerParams(dimension_semantics=sem, vmem_limit_bytes=VMEM_LIMIT_BYTES)


def _sigmoid(x):
    return 1.0 / (1.0 + jnp.exp(-x))


def _softplus(x):
    return jnp.maximum(x, 0.0) + jnp.log1p(jnp.exp(-jnp.abs(x)))


def _log_sigmoid(x):
    return -_softplus(-x)


def _gelu_tanh(x):
    return 0.5 * x * (1.0 + jnp.tanh(0.7978845608028654 * (x + 0.044715 * (x * x * x))))


def _bdot(a, b):
    return jnp.dot(a.astype(BF16), b.astype(BF16), preferred_element_type=F32)


def _rmsnorm(x, g):
    return x * lax.rsqrt(jnp.mean(x * x, axis=-1, keepdims=True) + EPS) * g


def _split_dot(x, m01, parts):
    acc = None
    r = x
    for p in range(parts):
        h = r.astype(BF16)
        d = jnp.dot(h, m01, preferred_element_type=F32)
        acc = d if acc is None else acc + d
        if p + 1 < parts:
            r = r - h.astype(F32)
    return acc


def _split_dot_lhs01(m01, x, parts):
    acc = None
    r = x
    for p in range(parts):
        h = r.astype(BF16)
        d = jnp.dot(m01, h, preferred_element_type=F32)
        acc = d if acc is None else acc + d
        if p + 1 < parts:
            r = r - h.astype(F32)
    return acc


_NN = (((1,), (0,)), ((), ()))
_NT = (((1,), (1,)), ((), ()))


def _dot_pieces(a, b, dims):
    a_hi, b_hi = a.astype(BF16), b.astype(BF16)
    a_lo, b_lo = (a - a_hi.astype(F32)).astype(BF16), (b - b_hi.astype(F32)).astype(BF16)
    dot = lambda x, y: lax.dot_general(x, y, dims, preferred_element_type=F32)
    return dot(a_hi, b_hi) + dot(a_hi, b_lo) + dot(a_lo, b_hi)


def _head_ones(n):
    r = lax.broadcasted_iota(jnp.int32, (n, n), 0) // R_HEAD_DIM
    c = lax.broadcasted_iota(jnp.int32, (n, n), 1) // R_HEAD_DIM
    return jnp.where(r == c, 1.0, 0.0).astype(BF16)


def _head_sum64(x, ones):
    wd = ones.shape[0]
    outs = [_split_dot(x[:, s:s + wd], ones, 2) for s in range(0, x.shape[1], wd)]
    return outs[0] if len(outs) == 1 else jnp.concatenate(outs, axis=1)


def _shift_rows(x, tail, s):
    n, rt = x.shape[0], tail.shape[0]
    if s % SUBLANES == 0:
        if s == n:
            return tail[rt - s:]
        return jnp.concatenate([tail[rt - s:], x[:n - s]], axis=0)
    assert s < SUBLANES
    r = pltpu.roll(x, s, 0)
    t8 = pltpu.roll(tail[rt - SUBLANES:], s, 0)
    row = lax.broadcasted_iota(jnp.int32, (SUBLANES, x.shape[1]), 0)
    first = jnp.where(row < s, t8, r[:SUBLANES])
    if n == SUBLANES:
        return first
    return jnp.concatenate([first, r[SUBLANES:]], axis=0)


def _front_kernel(x_ref, mst_ref, rst_ref, n1g_ref, wm_ref, wr_ref, wg_ref, wif_ref, cw_ref, cb_ref, qsc_ref, mix_ref, w2_ref, a2_ref,
                  g2_ref, w0_ref, a0_ref, kk_ref, ka_ref, rk_ref,
                  qk_o, zv_o, zo_o, zg_o, zif_o, r_o, w_o, k_o, v_o, a_o, b_o, g_o, bonus_o, mtail_o, rtail_o,
                  mtail_scr, rtail_scr, *, nb):
    @pl.when(pl.program_id(1) == 0)
    def _():
        mtail_scr[...] = mst_ref[0]
        rtail_scr[...] = rst_ref[0]

    h = _rmsnorm(x_ref[0], n1g_ref[...]).astype(BF16)
    proj = lambda w: jnp.dot(h, w, preferred_element_type=F32)
    tr = h.shape[0]
    mw2 = 2 * M_WIDTH

    z = proj(wr_ref[...])
    rtail = rtail_scr[...]
    prev = _shift_rows(z, rtail, nb)
    rtail_scr[...] = z[tr - rtail.shape[0]:]
    xm = z + (prev - z) * mix_ref[...]
    w = R_WIDTH
    r, kr, vr = xm[:, 0:w], xm[:, w:2 * w], xm[:, 2 * w:3 * w]
    lora_in = xm[:, 3 * w:3 * w + LANES]
    gd = xm[:, 3 * w + LANES:3 * w + 2 * LANES]
    lw = _bdot(jnp.tanh(lora_in), w2_ref[...])
    la = _bdot(lora_in, a2_ref[...])
    wlog = -_softplus(-(w0_ref[...] + lw)) - 0.5
    a = _sigmoid(a0_ref[...] + la)
    ones = _head_ones(2 * LANES)
    kk = kr * kk_ref[...]
    kk = kk / jnp.maximum(jnp.sqrt(_head_sum64(kk * kk, ones)), 1e-12)
    k2 = kr * (1.0 + (a - 1.0) * ka_ref[...])
    r_o[0] = r
    w_o[0] = jnp.exp(-jnp.exp(wlog))
    k_o[0] = k2
    v_o[0] = vr
    a_o[0] = -kk
    b_o[0] = kk * a
    g_o[0] = _bdot(_sigmoid(gd), g2_ref[...]).astype(g_o.dtype)
    bonus_o[0] = (_head_sum64(r * k2 * rk_ref[...], ones) * vr).astype(bonus_o.dtype)

    zqk = proj(wm_ref[:, 0:mw2])
    mtail = mtail_scr[...]
    y = cb_ref[...] + cw_ref[M_CONV - 1:M_CONV, :] * zqk
    for j in range(M_CONV - 1):
        y = y + cw_ref[j:j + 1, :] * _shift_rows(zqk, mtail, (M_CONV - 1 - j) * nb)
    mtail_scr[...] = zqk[tr - mtail.shape[0]:]
    qk_o[0] = y * _sigmoid(y) * qsc_ref[...]
    zv_o[0] = proj(wm_ref[:, mw2:mw2 + M_WIDTH])
    zo_o[0] = proj(wm_ref[:, mw2 + M_WIDTH:mw2 + 2 * M_WIDTH])
    zg_o[0] = proj(wg_ref[...]).astype(zg_o.dtype)
    zif_o[0] = proj(wif_ref[...])

    @pl.when(pl.program_id(1) == pl.num_programs(1) - 1)
    def _():
        mtail_o[0] = mtail_scr[...]
        rtail_o[0] = rtail_scr[...]


def _front(x, mstate, rstate, wts, nb, tr, name):
    g, r, d = x.shape
    rt_m, rt_r = mstate.shape[1], rstate.shape[1]
    assert r % tr == 0 and tr >= max(rt_m, rt_r)
    mw2, w = 2 * M_WIDTH, R_WIDTH
    tok = lambda c: pl.BlockSpec((1, tr, c), lambda a, i: (a, i, 0))
    const = lambda shp: pl.BlockSpec(shp, lambda a, i: tuple(0 for _ in shp), pipeline_mode=pl.Buffered(1))
    tail = lambda rows, c: pl.BlockSpec((1, rows, c), lambda a, i: (a, 0, 0))
    out_cols = [(mw2, F32), (M_WIDTH, F32), (M_WIDTH, F32), (2 * d, BF16), (LANES, F32)] \
        + [(w, F32)] * 6 + [(w, BF16), (w, BF16)]
    return pl.pallas_call(
        functools.partial(_front_kernel, nb=nb),
        grid=(g, r // tr),
        in_specs=[tok(d), tail(rt_m, mw2), tail(rt_r, R_COLS), const((1, d)),
                  const((d, 2 * mw2)), const((d, R_COLS)), const((d, 2 * d)), const((d, LANES)),
                  const((SUBLANES, mw2)), const((1, mw2)), const((1, mw2)), const((1, R_COLS)),
                  const((LANES, w)), const((LANES, w)), const((LANES, w)),
                  const((1, w)), const((1, w)), const((1, w)), const((1, w)), const((1, w))],
        out_specs=[tok(c) for c, _ in out_cols] + [tail(rt_m, mw2), tail(rt_r, R_COLS)],
        out_shape=[jax.ShapeDtypeStruct((g, r, c), dt) for c, dt in out_cols]
        + [jax.ShapeDtypeStruct((g, rt_m, mw2), F32), jax.ShapeDtypeStruct((g, rt_r, R_COLS), F32)],
        scratch_shapes=[pltpu.VMEM((rt_m, mw2), F32), pltpu.VMEM((rt_r, R_COLS), F32)],
        compiler_params=_cparams("parallel", "arbitrary"),
        name=name,
    )(x, mstate, rstate, wts["norm1_g"], wts["w_m"], wts["w_r"], wts["w_g"], wts["w_if"], wts["m_conv_w"], wts["m_conv_b"], wts["m_qk_scale"],
      wts["r_mix"], wts["r_w2"], wts["r_a2"], wts["r_g2"], wts["r_w0"], wts["r_a0"], wts["r_kk"], wts["r_ka"],
      wts["r_rk"])


def _mlstm_chunk(qk, v, gates, ng_ref, state, lb, lp):
    hd, nh = M_HEAD_DIM, M_HEADS
    row = lax.broadcasted_iota(jnp.int32, (lp, LANES), 0)
    lane = lax.broadcasted_iota(jnp.int32, (lp, LANES), 1)
    valid = row < lb
    li = jnp.where(valid, gates, -jnp.inf)
    lf = jnp.where(valid & (lane >= nh) & (lane < 2 * nh), _log_sigmoid(gates), 0.0)
    sq_r = lax.broadcasted_iota(jnp.int32, (lp, lp), 0)
    sq_c = lax.broadcasted_iota(jnp.int32, (lp, lp), 1)
    tril = jnp.where(sq_r >= sq_c, 1.0, 0.0).astype(BF16)
    bcum = _split_dot_lhs01(tril, lf, 2)
    xg = jnp.where(lane < nh, li, bcum)
    xg_t = xg.T

    keep = sq_r <= sq_c
    ct, n, m = state
    outs, ct_new, n_new, m_new = [], [], [], []
    for h in range(nh):
        brow = xg_t[nh + h:nh + h + 1, :]
        irow = xg_t[h:h + 1, :]
        ccol = xg[:, h:h + 1] - xg[:, nh + h:nh + h + 1]
        q = qk[:, h * hd:(h + 1) * hd]
        k = qk[:, M_WIDTH + h * hd:M_WIDTH + (h + 1) * hd]
        qb, kb = q.astype(BF16), k.astype(BF16)
        v_t = v[:, h * hd:(h + 1) * hd].T

        dm = jnp.where(keep, brow + ccol, -jnp.inf)
        inter = brow + m[h]
        mt = jnp.maximum(inter, jnp.max(dm, axis=0, keepdims=True))
        s_t = lax.dot_general(kb, qb, _NT, preferred_element_type=F32)
        p_t = jnp.exp(dm - mt) * s_t
        sc = jnp.exp(inter - mt)
        num = sc * lax.dot_general(ct[h].astype(BF16), qb, _NT, preferred_element_type=F32) \
            + jnp.dot(v_t.astype(BF16), p_t.astype(BF16), preferred_element_type=F32)
        qn = _dot_pieces(jnp.broadcast_to(n[h], (SUBLANES, hd)), q, _NT)[0:1, :]
        den = sc * qn + jnp.sum(p_t, axis=0, keepdims=True)
        hh = num / jnp.maximum(jnp.abs(den), jnp.exp(-mt))
        mu = jnp.mean(hh, axis=0, keepdims=True)
        d = hh - mu
        outs.append(d * lax.rsqrt(jnp.mean(d * d, axis=0, keepdims=True) + EPS) * ng_ref[h, :, 0:lp])

        btot = brow[:, lp - 1:lp]
        wlog = btot - brow + irow
        m_h = jnp.maximum(btot + m[h], jnp.max(wlog, axis=1, keepdims=True))
        s0 = jnp.exp(btot + m[h] - m_h)
        ws = jnp.exp(wlog - m_h)
        ct_new.append(s0 * ct[h] + jnp.dot((v_t * ws).astype(BF16), kb, preferred_element_type=F32))
        n_new.append(s0 * n[h] + _dot_pieces(jnp.broadcast_to(ws, (SUBLANES, lp)), k, _NN)[0:1, :])
        m_new.append(m_h)
    return outs, (ct_new, n_new, m_new)


def _mlstm_kernel(qk_ref, v_ref, og_ref, gt_ref, gb_ref, ng_ref, c0_ref, n0_ref, m0_ref,
                  h_ref, ct_ref, nt_ref, mt_ref, c_scr, n_scr, m_scr, *, lb, nsub, bb):
    lp, hd, nh = max(lb, MLSTM_MIN_CHUNK), M_HEAD_DIM, M_HEADS

    @pl.when(pl.program_id(1) == 0)
    def _():
        for si in range(bb):
            for h in range(nh):
                c_scr[si, h] = c0_ref[si, h].T
        n_scr[...] = n0_ref[...]
        m_scr[...] = m0_ref[...]

    def pad(x):
        if lb == lp:
            return x
        return jnp.concatenate([x, jnp.zeros((lp - lb, x.shape[1]), x.dtype)], axis=0)

    for si in range(bb):
        n_all, m_all = n_scr[si], m_scr[si]
        state = ([c_scr[si, h] for h in range(nh)], [n_all[h:h + 1, :] for h in range(nh)],
                 [m_all[h:h + 1, 0:1] for h in range(nh)])
        for ci in range(nsub):
            rows = slice(ci * lb, (ci + 1) * lb)
            outs, state = _mlstm_chunk(pad(qk_ref[si, rows, :]), pad(v_ref[si, rows, :]),
                                       pad(gt_ref[si, rows, :]) + gb_ref[...], ng_ref, state, lb, lp)
            for h in range(nh):
                cols = slice(h * hd, (h + 1) * hd)
                h_ref[si, rows, cols] = _sigmoid(og_ref[si, rows, cols]) * outs[h].T[:lb]
        ct, n, m = state
        for h in range(nh):
            c_scr[si, h] = ct[h]
        n_scr[si, 0:nh, :] = jnp.concatenate(n, axis=0)
        m_scr[si, 0:nh, :] = jnp.concatenate([jnp.broadcast_to(x, (1, LANES)) for x in m], axis=0)

    @pl.when(pl.program_id(1) == pl.num_programs(1) - 1)
    def _():
        for si in range(bb):
            for h in range(nh):
                ct_ref[si, h] = c_scr[si, h].T
        nt_ref[...] = n_scr[...]
        mt_ref[...] = m_scr[...]


def _mlstm(qk, v, og, gt, gbias, ng, c0, n0, m0, name):
    b, t, _ = qk.shape
    lb = min(t, MLSTM_CHUNK)
    nsub = min(t // lb, MLSTM_SUBCHUNKS)
    bb = 1 if nsub > 1 else min(b, MLSTM_SHORT_SEQS)
    assert t % (lb * nsub) == 0 and b % bb == 0
    tok = lambda w: pl.BlockSpec((bb, lb * nsub, w), lambda a, c: (a, c, 0))
    const = lambda shp: pl.BlockSpec(shp, lambda a, c: tuple(0 for _ in shp))
    st4 = pl.BlockSpec((bb, M_HEADS, M_HEAD_DIM, M_HEAD_DIM), lambda a, c: (a, 0, 0, 0))
    st3 = pl.BlockSpec((bb, SUBLANES, LANES), lambda a, c: (a, 0, 0))
    return pl.pallas_call(
        functools.partial(_mlstm_kernel, lb=lb, nsub=nsub, bb=bb),
        grid=(b // bb, t // (lb * nsub)),
        in_specs=[tok(2 * M_WIDTH), tok(M_WIDTH), tok(M_WIDTH), tok(LANES),
                  const((1, LANES)), const((M_HEADS, M_HEAD_DIM, MLSTM_CHUNK)), st4, st3, st3],
        out_specs=[tok(M_WIDTH), st4, st3, st3],
        out_shape=[jax.ShapeDtypeStruct((b, t, M_WIDTH), F32),
                   jax.ShapeDtypeStruct((b, M_HEADS, M_HEAD_DIM, M_HEAD_DIM), F32),
                   jax.ShapeDtypeStruct((b, SUBLANES, LANES), F32),
                   jax.ShapeDtypeStruct((b, SUBLANES, LANES), F32)],
        scratch_shapes=[pltpu.VMEM((bb, M_HEADS, M_HEAD_DIM, M_HEAD_DIM), F32),
                        pltpu.VMEM((bb, SUBLANES, LANES), F32),
                        pltpu.VMEM((bb, SUBLANES, LANES), F32)],
        compiler_params=_cparams("parallel", "arbitrary"),
        name=name,
    )(qk, v, og, gt, gbias, ng, c0, n0, m0)


def _rscan_kernel(r_ref, w_ref, k_ref, v_ref, a_ref, b_ref, s0_ref, y_ref, st_ref, s_scr, y_scr, *, bg, steps):
    hp = R_HEADS // 2
    npair = bg * hp
    nquad = npair // 2
    hd = R_HEAD_DIM
    tc = pl.program_id(1)
    pairs = [(bi, p) for bi in range(bg) for p in range(hp)]
    sl = lambda q: slice(q * hd, (q + 1) * hd)

    @pl.when(tc == 0)
    def _():
        for q, (bi, p) in enumerate(pairs):
            s_scr[sl(q), :] = jnp.concatenate([s0_ref[bi, 2 * p], s0_ref[bi, 2 * p + 1]], axis=1)

    y_scr[...] = jnp.zeros(y_scr.shape, F32)
    ones = _head_ones(2 * LANES)
    lane = lax.broadcasted_iota(jnp.int32, (hd, LANES), 1) & (hd - 1)
    diag = lane == lax.broadcasted_iota(jnp.int32, (hd, LANES), 0)

    def side_by_side(xs):
        return jnp.concatenate([jnp.concatenate([xs[2 * u], xs[2 * u + 1]], axis=1) for u in range(nquad)], axis=0)

    def pair_of(res, q):
        return res[(q // 2) * hd:(q // 2 + 1) * hd, (q % 2) * LANES:(q % 2 + 1) * LANES]

    def step8(t8, carry):
        tiles = [[ref[bi, t8, :, p * LANES:(p + 1) * LANES] for bi, p in pairs]
                 for ref in (r_ref, w_ref, k_ref, v_ref, a_ref, b_ref)]
        s_cur = [s_scr[sl(q), :] for q in range(npair)]
        for j in range(SUBLANES):
            r_t, w_t, k_t, v_t, a_t, b_t = [[tl[q][j:j + 1, :] for q in range(npair)] for tl in tiles]
            prod = side_by_side([s_cur[q] * a_t[q] for q in range(npair)])
            sa = _split_dot(prod, ones, SCAN_SA_PIECES)
            vdiag = side_by_side([jnp.where(diag, v_t[q], 0.0) for q in range(npair)])
            vcol = jnp.dot(vdiag.astype(BF16), ones, preferred_element_type=F32)
            s_cur = [s_cur[q] * w_t[q] + pair_of(sa, q) * b_t[q] + pair_of(vcol, q) * k_t[q] for q in range(npair)]
            outs = side_by_side([s_cur[q] * r_t[q] for q in range(npair)])
            yb = jnp.dot(outs.astype(BF16), ones, preferred_element_type=F32)
            hit = lane == t8 * SUBLANES + j
            for q in range(npair):
                pltpu.store(y_scr.at[sl(q), :], pair_of(yb, q), mask=hit)
        for q in range(npair):
            s_scr[sl(q), :] = s_cur[q]
        return carry

    lax.fori_loop(0, steps // SUBLANES, step8, 0)

    low = lax.broadcasted_iota(jnp.int32, (hd, LANES), 1) < hd
    for u in range(nquad):
        yt = y_scr[u * LANES:(u + 1) * LANES, :].T
        top, bot = yt[:hd], yt[hd:]
        bi, p0 = pairs[2 * u]
        y_ref[bi, :, p0 * LANES:(p0 + 1) * LANES] = jnp.where(low, top, pltpu.roll(bot, hd, 1))[:steps]
        y_ref[bi, :, (p0 + 1) * LANES:(p0 + 2) * LANES] = jnp.where(low, pltpu.roll(top, hd, 1), bot)[:steps]

    @pl.when(tc == pl.num_programs(1) - 1)
    def _():
        for q, (bi, p) in enumerate(pairs):
            s_fin = s_scr[sl(q), :]
            st_ref[bi, 2 * p] = s_fin[:, :hd]
            st_ref[bi, 2 * p + 1] = s_fin[:, hd:]


def _rscan(ops, s0, name):
    nbatch, nsteps, _ = ops[0].shape
    bg = SCAN_BATCH_GROUP
    steps = min(nsteps, SCAN_CHUNK)
    assert nbatch % bg == 0 and nsteps % steps == 0 and steps % SUBLANES == 0
    rows = bg * (R_HEADS // 2) * R_HEAD_DIM
    ops = [o.reshape(nbatch, nsteps // SUBLANES, SUBLANES, R_WIDTH) for o in ops]
    opspec = pl.BlockSpec((bg, steps // SUBLANES, SUBLANES, R_WIDTH), lambda g, c: (g, c, 0, 0))
    stspec = pl.BlockSpec((bg, R_HEADS, R_HEAD_DIM, R_HEAD_DIM), lambda g, c: (g, 0, 0, 0))
    return pl.pallas_call(
        functools.partial(_rscan_kernel, bg=bg, steps=steps),
        grid=(nbatch // bg, nsteps // steps),
        in_specs=[opspec] * 6 + [stspec],
        out_specs=[pl.BlockSpec((bg, steps, R_WIDTH), lambda g, c: (g, c, 0)), stspec],
        out_shape=[jax.ShapeDtypeStruct((nbatch, nsteps, R_WIDTH), F32),
                   jax.ShapeDtypeStruct(s0.shape, F32)],
        scratch_shapes=[pltpu.VMEM((rows, LANES), F32), pltpu.VMEM((rows, LANES), F32)],
        compiler_params=_cparams("parallel", "arbitrary"),
        name=name,
    )(*ops, s0)


def _merge_kernel(x_ref, hm_ref, y_ref, bonus_ref, g_ref, zg_ref, lng_ref, lnb_ref,
                  wbm_ref, wbr_ref, wo_ref, o_ref):
    ones = _head_ones(2 * LANES)
    y = y_ref[...]
    inv = 1.0 / R_HEAD_DIM
    mu = _head_sum64(y, ones) * inv
    d = y - mu
    var = _head_sum64(d * d, ones) * inv
    yn = d * lax.rsqrt(var + GN_EPS) * lng_ref[...] + lnb_ref[...]
    yr = (yn + bonus_ref[...].astype(F32)) * g_ref[...].astype(F32)
    y_b = _bdot(yr, wbr_ref[...])
    y_a = _bdot(hm_ref[...], wbm_ref[...])
    dm = x_ref.shape[1]
    merged = _sigmoid(zg_ref[:, 0:dm].astype(F32)) * y_a + _sigmoid(zg_ref[:, dm:2 * dm].astype(F32)) * y_b
    o_ref[...] = x_ref[...] + _bdot(merged, wo_ref[...])


def _merge(x, hm, y, bonus, g, zg, lng, lnb, wbm, wbr, wo, tm, name):
    n, d = x.shape
    w = R_WIDTH
    tok = lambda c: pl.BlockSpec((tm, c), lambda i: (i, 0))
    const = lambda shp: pl.BlockSpec(shp, lambda i: tuple(0 for _ in shp))
    return pl.pallas_call(
        _merge_kernel,
        grid=(n // tm,),
        in_specs=[tok(d), tok(M_WIDTH), tok(w), tok(w), tok(w), tok(2 * d),
                  const((1, w)), const((1, w)),
                  const((M_WIDTH, d)), const((w, d)), const((d, d))],
        out_specs=tok(d),
        out_shape=jax.ShapeDtypeStruct((n, d), F32),
        compiler_params=_cparams("parallel"),
        name=name,
    )(x, hm, y, bonus, g, zg, lng, lnb, wbm, wbr, wo)


def _ffn_kernel(x_ref, pe_ref, st_ref, n2g_ref, wup_ref, cw_ref, cb_ref, fd_ref, png_ref, pgw_ref,
                ppj_ref, fng_ref, y_ref, tail_ref, tail_scr, *, nb, rt, dff):
    @pl.when(pl.program_id(1) == 0)
    def _():
        tail_scr[...] = st_ref[0]

    x1 = x_ref[0]
    h2 = _rmsnorm(x1, n2g_ref[...]).astype(BF16)
    uc = jnp.dot(h2, wup_ref[:, 0:dff], preferred_element_type=F32)
    tail = tail_scr[...]
    c = cb_ref[...] + cw_ref[F_CONV - 1:F_CONV, :] * uc
    for j in range(F_CONV - 1):
        c = c + cw_ref[j:j + 1, :] * _shift_rows(uc, tail, (F_CONV - 1 - j) * nb)
    tail_scr[...] = uc[uc.shape[0] - rt:]
    ug = jnp.dot(h2, wup_ref[:, dff:2 * dff], preferred_element_type=F32)
    x2 = x1 + _bdot(_gelu_tanh(c) * ug, fd_ref[...])
    gate = _sigmoid(_bdot(_rmsnorm(x2, png_ref[...]), pgw_ref[...]))
    x3 = x2 + gate * _bdot(pe_ref[0], ppj_ref[...])
    y_ref[0] = _rmsnorm(x3, fng_ref[...])

    @pl.when(pl.program_id(1) == pl.num_programs(1) - 1)
    def _():
        tail_ref[0] = tail_scr[...]


def _ffn(x, pe, state, n2g, wup, cw8, cb, fd, png, pgw, ppj, fng, nb, tr, name):
    g, r, d = x.shape
    dff = fd.shape[0]
    pdim = pe.shape[2]
    rt = state.shape[1]
    assert r % tr == 0 and tr >= rt
    tok = lambda c: pl.BlockSpec((1, tr, c), lambda a, i: (a, i, 0))
    const = lambda shp: pl.BlockSpec(shp, lambda a, i: tuple(0 for _ in shp), pipeline_mode=pl.Buffered(1))
    tailspec = pl.BlockSpec((1, rt, dff), lambda a, i: (a, 0, 0))
    return pl.pallas_call(
        functools.partial(_ffn_kernel, nb=nb, rt=rt, dff=dff),
        grid=(g, r // tr),
        in_specs=[tok(d), tok(pdim), tailspec, const((1, d)), const((d, 2 * dff)),
                  const((SUBLANES, dff)), const((1, dff)), const((dff, d)), const((1, d)), const((d, d)),
                  const((pdim, d)), const((1, d))],
        out_specs=[tok(d), tailspec],
        out_shape=[jax.ShapeDtypeStruct((g, r, d), F32), jax.ShapeDtypeStruct((g, rt, dff), F32)],
        scratch_shapes=[pltpu.VMEM((rt, dff), F32)],
        compiler_params=_cparams("parallel", "arbitrary"),
        name=name,
    )(x, pe, state, n2g, wup, cw8, cb, fd, png, pgw, ppj, fng)


def _pad_rows(w, rows):
    return jnp.pad(w, ((0, rows - w.shape[0]), (0, 0)))


def _layer(x, pe, st, wts, nb):
    g, r, d = x.shape
    n = g * r
    nseq = g * nb
    nsteps = r // nb
    xf = x.reshape(n, d)
    flat = lambda a: a.reshape(n, -1)

    tr = FRONT_ROWS
    (qk, zv, zo, zg, zif, rr, rw, rk, rv, ra, rb, rg, bonus, mtail, rtail) = _front(
        x, st["mconv"], st["rshift"], wts, nb, tr, "front")

    to_seq = (lambda a: a.reshape(nseq, nsteps, a.shape[-1])) if nb == 1 else \
        (lambda a: a.reshape(nsteps, nseq, a.shape[-1]).transpose(1, 0, 2))
    from_seq = (lambda a: a.reshape(n, -1)) if nb == 1 else (lambda a: a.transpose(1, 0, 2).reshape(n, -1))

    hm, mc, mn, mm = _mlstm(to_seq(flat(qk)), to_seq(flat(zv)), to_seq(flat(zo)), to_seq(flat(zif)),
                            wts["m_gate_bias"], wts["m_norm_g"], st["mC"], st["mn"], st["mm"], "mlstm")

    y, s_new = _rscan([to_seq(flat(o)) for o in (rr, rw, rk, rv, ra, rb)], st["rS"], "rscan")

    x1 = _merge(xf, from_seq(hm), from_seq(y), flat(bonus), flat(rg), flat(zg), wts["r_ln_g"], wts["r_ln_b"],
                wts["w_branch_m"], wts["w_branch_r"], wts["w_out"], MERGE_ROWS, "merge")

    tr_f = max(FFN_ROWS, st["fconv"].shape[1])
    yout, utail = _ffn(x1.reshape(g, r, d), pe, st["fconv"], wts["norm2_g"], wts["f_up"], wts["f_conv_w"],
                       wts["f_conv_b"], wts["f_down"], wts["ple_norm_g"], wts["ple_gate_w"], wts["ple_proj"],
                       wts["final_norm_g"], nb, tr_f, "ffn")
    return yout, dict(mtail=mtail, rtail=rtail, utail=utail, mC=mc, mn=mn, mm=mm, rS=s_new)


def kernel(x_prompt, x_sample, state_mlstm_conv, state_mlstm_C, state_mlstm_n, state_mlstm_m,
           state_rwkv_shift, state_rwkv_S, state_ffn_conv, p_prompt, p_sample,
           norm1_g, w_in, m_conv_w, m_conv_b, m_i_bias, m_f_bias, m_norm_g, w_branch_m,
           r_mix, r_w0, r_w2, r_a0, r_a2, r_g2, r_kk, r_ka, r_rk, r_ln_g, r_ln_b, w_branch_r,
           w_out, norm2_g, f_up, f_conv_w, f_conv_b, f_down, ple_norm_g, ple_gate_w, ple_proj,
           final_norm_g):
    assert norm1_g.shape[0] == 1, "single trunk layer"
    bp, tp, d = x_prompt.shape
    bs, ts, _ = x_sample.shape
    dff = f_down.shape[1]
    mw2 = 2 * M_WIDTH
    row = lambda a: a.reshape(1, -1).astype(F32)

    wi = w_in[0]
    c_if = 4 * M_WIDTH
    c_r = c_if + 2 * M_HEADS
    c_g = c_r + R_COLS
    lora = jnp.zeros((LANES, R_WIDTH), F32)
    wts = dict(
        norm1_g=row(norm1_g[0]),
        w_m=wi[:, :c_if].astype(BF16), w_r=wi[:, c_r:c_g].astype(BF16), w_g=wi[:, c_g:].astype(BF16),
        w_if=jnp.pad(wi[:, c_if:c_r], ((0, 0), (0, LANES - 2 * M_HEADS))).astype(BF16),
        m_conv_w=_pad_rows(m_conv_w[0], SUBLANES), m_conv_b=row(m_conv_b[0]),
        m_qk_scale=jnp.concatenate([jnp.ones((1, M_WIDTH), F32),
                                    jnp.full((1, M_WIDTH), M_HEAD_DIM ** -0.5, F32)], axis=1),
        m_gate_bias=jnp.pad(jnp.concatenate([m_i_bias[0], m_f_bias[0]]).reshape(1, -1),
                            ((0, 0), (0, LANES - 2 * M_HEADS))),
        m_norm_g=jnp.broadcast_to(m_norm_g[0].reshape(M_HEADS, M_HEAD_DIM, 1), (M_HEADS, M_HEAD_DIM, MLSTM_CHUNK)),
        w_branch_m=w_branch_m[0].astype(BF16),
        r_mix=row(r_mix[0]), r_w0=row(r_w0[0]), r_a0=row(r_a0[0]),
        r_w2=lora.at[:R_DECAY_LORA].set(r_w2[0]).astype(BF16),
        r_a2=lora.at[R_DECAY_LORA:R_DECAY_LORA + R_A_LORA].set(r_a2[0]).astype(BF16),
        r_g2=r_g2[0].astype(BF16), r_kk=row(r_kk[0]), r_ka=row(r_ka[0]), r_rk=row(r_rk[0]),
        r_ln_g=row(r_ln_g[0]), r_ln_b=row(r_ln_b[0]), w_branch_r=w_branch_r[0].astype(BF16),
        w_out=w_out[0].astype(BF16), norm2_g=row(norm2_g[0]), f_up=f_up[0].astype(BF16),
        f_conv_w=_pad_rows(f_conv_w[0], SUBLANES), f_conv_b=row(f_conv_b[0]),
        f_down=f_down[0].astype(BF16), ple_norm_g=row(ple_norm_g[0]), ple_gate_w=ple_gate_w[0].astype(BF16),
        ple_proj=ple_proj[0].astype(BF16), final_norm_g=row(final_norm_g),
    )

    def seq_states(c, nv, m):
        b = c.shape[0]
        nv8 = jnp.pad(nv, ((0, 0), (0, SUBLANES - M_HEADS), (0, 0)))
        m8 = jnp.pad(jnp.broadcast_to(m[:, :, None], (b, M_HEADS, LANES)), ((0, 0), (0, SUBLANES - M_HEADS), (0, 0)))
        return c, nv8, m8

    zc, zn, zm = seq_states(jnp.zeros((bp, M_HEADS, M_HEAD_DIM, M_HEAD_DIM), F32),
                            jnp.zeros((bp, M_HEADS, M_HEAD_DIM), F32), jnp.zeros((bp, M_HEADS), F32))
    st_p = dict(mconv=jnp.zeros((bp, SUBLANES, mw2), F32), rshift=jnp.zeros((bp, SUBLANES, R_COLS), F32),
                fconv=jnp.zeros((bp, SUBLANES, dff), F32), mC=zc, mn=zn, mm=zm,
                rS=jnp.zeros((bp, R_HEADS, R_HEAD_DIM, R_HEAD_DIM), F32))
    y_p, new_p = _layer(x_prompt, p_prompt[0], st_p, wts, 1)

    tmaj = lambda a: a.transpose(1, 0, 2).reshape(1, a.shape[0] * a.shape[1], a.shape[2])
    sc, sn, sm = seq_states(state_mlstm_C[0], state_mlstm_n[0], state_mlstm_m[0])
    st_s = dict(mconv=tmaj(state_mlstm_conv[0]), rshift=state_rwkv_shift[0][None], fconv=tmaj(state_ffn_conv[0]),
                mC=sc, mn=sn, mm=sm, rS=state_rwkv_S[0])
    y_s, new_s = _layer(tmaj(x_sample), tmaj(p_sample[0]), st_s, wts, bs)
    y_sample = y_s.reshape(ts, bs, d).transpose(1, 0, 2)

    def finish(new, b, tmajor):
        def last(tail, k):
            if tmajor:
                return tail.reshape(-1, b, tail.shape[-1])[-k:].transpose(1, 0, 2)
            return tail[:, tail.shape[1] - k:]
        return (last(new["mtail"], M_CONV - 1)[None], new["mC"][None], new["mn"][:, :M_HEADS][None],
                new["mm"][:, :M_HEADS, 0][None], last(new["rtail"], 1)[:, 0][None],
                new["rS"][None], last(new["utail"], F_CONV - 1)[None])

    return (y_p, y_sample) + finish(new_p, bp, False) + finish(new_s, bs, True)
```

```python
import functools

import jax
import jax.numpy as jnp
from jax import lax
from jax.experimental import pallas as pl
from jax.experimental.pallas import tpu as pltpu

F32 = jnp.float32
BF16 = jnp.bfloat16

M_HEADS = 4
M_HEAD_DIM = 128
M_WIDTH = M_HEADS * M_HEAD_DIM
M_CONV = 4
R_HEADS = 8
R_HEAD_DIM = 64
R_WIDTH = R_HEADS * R_HEAD_DIM
R_DECAY_LORA = 64
R_A_LORA = 64
R_GATE_LORA = 128
R_COLS = 3 * R_WIDTH + R_DECAY_LORA + R_A_LORA + R_GATE_LORA
F_CONV = 3
EPS = 1e-6
GN_EPS = 64e-5

LANES = 128
SUBLANES = 8
VMEM_LIMIT_BYTES = 56 * 1024 * 1024

FRONT_ROWS = 512
MERGE_ROWS = 512
FFN_ROWS = 512
MLSTM_CHUNK = 256
MLSTM_MIN_CHUNK = 128
MLSTM_SUBCHUNKS = 8
MLSTM_SHORT_SEQS = 8
SCAN_CHUNK = 64
SCAN_BATCH_GROUP = 8
SCAN_SA_PIECES = 1


def _cparams(*sem):
    return pltpu.CompilerParams(dimension_semantics=sem, vmem_limit_bytes=VMEM_LIMIT_BYTES)


def _sigmoid(x):
    return 1.0 / (1.0 + jnp.exp(-x))


def _softplus(x):
    return jnp.maximum(x, 0.0) + jnp.log1p(jnp.exp(-jnp.abs(x)))


def _log_sigmoid(x):
    return -_softplus(-x)


def _gelu_tanh(x):
    return 0.5 * x * (1.0 + jnp.tanh(0.7978845608028654 * (x + 0.044715 * (x * x * x))))


def _bdot(a, b):
    return jnp.dot(a.astype(BF16), b.astype(BF16), preferred_element_type=F32)


def _rmsnorm(x, g):
    return x * lax.rsqrt(jnp.mean(x * x, axis=-1, keepdims=True) + EPS) * g


def _split_dot(x, m01, parts):
    acc = None
    r = x
    for p in range(parts):
        h = r.astype(BF16)
        d = jnp.dot(h, m01, preferred_element_type=F32)
        acc = d if acc is None else acc + d
        if p + 1 < parts:
            r = r - h.astype(F32)
    return acc


def _split_dot_lhs01(m01, x, parts):
    acc = None
    r = x
    for p in range(parts):
        h = r.astype(BF16)
        d = jnp.dot(m01, h, preferred_element_type=F32)
        acc = d if acc is None else acc + d
        if p + 1 < parts:
            r = r - h.astype(F32)
    return acc


_NN = (((1,), (0,)), ((), ()))
_NT = (((1,), (1,)), ((), ()))


def _dot_pieces(a, b, dims):
    a_hi, b_hi = a.astype(BF16), b.astype(BF16)
    a_lo, b_lo = (a - a_hi.astype(F32)).astype(BF16), (b - b_hi.astype(F32)).astype(BF16)
    dot = lambda x, y: lax.dot_general(x, y, dims, preferred_element_type=F32)
    return dot(a_hi, b_hi) + dot(a_hi, b_lo) + dot(a_lo, b_hi)


def _head_ones(n):
    r = lax.broadcasted_iota(jnp.int32, (n, n), 0) // R_HEAD_DIM
    c = lax.broadcasted_iota(jnp.int32, (n, n), 1) // R_HEAD_DIM
    return jnp.where(r == c, 1.0, 0.0).astype(BF16)


def _head_sum64(x, ones):
    wd = ones.shape[0]
    outs = [_split_dot(x[:, s:s + wd], ones, 2) for s in range(0, x.shape[1], wd)]
    return outs[0] if len(outs) == 1 else jnp.concatenate(outs, axis=1)


def _shift_rows(x, tail, s):
    n, rt = x.shape[0], tail.shape[0]
    if s % SUBLANES == 0:
        if s == n:
            return tail[rt - s:]
        return jnp.concatenate([tail[rt - s:], x[:n - s]], axis=0)
    assert s < SUBLANES
    r = pltpu.roll(x, s, 0)
    t8 = pltpu.roll(tail[rt - SUBLANES:], s, 0)
    row = lax.broadcasted_iota(jnp.int32, (SUBLANES, x.shape[1]), 0)
    first = jnp.where(row < s, t8, r[:SUBLANES])
    if n == SUBLANES:
        return first
    return jnp.concatenate([first, r[SUBLANES:]], axis=0)


def _front_kernel(x_ref, mst_ref, rst_ref, n1g_ref, wm_ref, wr_ref, wg_ref, wif_ref, cw_ref, cb_ref, qsc_ref, mix_ref, w2_ref, a2_ref,
                  g2_ref, w0_ref, a0_ref, kk_ref, ka_ref, rk_ref,
                  qk_o, zv_o, zo_o, zg_o, zif_o, r_o, w_o, k_o, v_o, a_o, b_o, g_o, bonus_o, mtail_o, rtail_o,
                  mtail_scr, rtail_scr, *, nb):
    @pl.when(pl.program_id(1) == 0)
    def _():
        mtail_scr[...] = mst_ref[0]
        rtail_scr[...] = rst_ref[0]

    h = _rmsnorm(x_ref[0], n1g_ref[...]).astype(BF16)
    proj = lambda w: jnp.dot(h, w, preferred_element_type=F32)
    tr = h.shape[0]
    mw2 = 2 * M_WIDTH

    z = proj(wr_ref[...])
    rtail = rtail_scr[...]
    prev = _shift_rows(z, rtail, nb)
    rtail_scr[...] = z[tr - rtail.shape[0]:]
    xm = z + (prev - z) * mix_ref[...]
    w = R_WIDTH
    r, kr, vr = xm[:, 0:w], xm[:, w:2 * w], xm[:, 2 * w:3 * w]
    lora_in = xm[:, 3 * w:3 * w + LANES]
    gd = xm[:, 3 * w + LANES:3 * w + 2 * LANES]
    lw = _bdot(jnp.tanh(lora_in), w2_ref[...])
    la = _bdot(lora_in, a2_ref[...])
    wlog = -_softplus(-(w0_ref[...] + lw)) - 0.5
    a = _sigmoid(a0_ref[...] + la)
    ones = _head_ones(2 * LANES)
    kk = kr * kk_ref[...]
    kk = kk / jnp.maximum(jnp.sqrt(_head_sum64(kk * kk, ones)), 1e-12)
    k2 = kr * (1.0 + (a - 1.0) * ka_ref[...])
    r_o[0] = r
    w_o[0] = jnp.exp(-jnp.exp(wlog))
    k_o[0] = k2
    v_o[0] = vr
    a_o[0] = -kk
    b_o[0] = kk * a
    g_o[0] = _bdot(_sigmoid(gd), g2_ref[...]).astype(g_o.dtype)
    bonus_o[0] = (_head_sum64(r * k2 * rk_ref[...], ones) * vr).astype(bonus_o.dtype)

    zqk = proj(wm_ref[:, 0:mw2])
    mtail = mtail_scr[...]
    y = cb_ref[...] + cw_ref[M_CONV - 1:M_CONV, :] * zqk
    for j in range(M_CONV - 1):
        y = y + cw_ref[j:j + 1, :] * _shift_rows(zqk, mtail, (M_CONV - 1 - j) * nb)
    mtail_scr[...] = zqk[tr - mtail.shape[0]:]
    qk_o[0] = y * _sigmoid(y) * qsc_ref[...]
    zv_o[0] = proj(wm_ref[:, mw2:mw2 + M_WIDTH])
    zo_o[0] = proj(wm_ref[:, mw2 + M_WIDTH:mw2 + 2 * M_WIDTH])
    zg_o[0] = proj(wg_ref[...]).astype(zg_o.dtype)
    zif_o[0] = proj(wif_ref[...])

    @pl.when(pl.program_id(1) == pl.num_programs(1) - 1)
    def _():
        mtail_o[0] = mtail_scr[...]
        rtail_o[0] = rtail_scr[...]


def _front(x, mstate, rstate, wts, nb, tr, name):
    g, r, d = x.shape
    rt_m, rt_r = mstate.shape[1], rstate.shape[1]
    assert r % tr == 0 and tr >= max(rt_m, rt_r)
    mw2, w = 2 * M_WIDTH, R_WIDTH
    tok = lambda c: pl.BlockSpec((1, tr, c), lambda a, i: (a, i, 0))
    const = lambda shp: pl.BlockSpec(shp, lambda a, i: tuple(0 for _ in shp), pipeline_mode=pl.Buffered(1))
    tail = lambda rows, c: pl.BlockSpec((1, rows, c), lambda a, i: (a, 0, 0))
    out_cols = [(mw2, F32), (M_WIDTH, F32), (M_WIDTH, F32), (2 * d, BF16), (LANES, F32)] \
        + [(w, F32)] * 6 + [(w, BF16), (w, BF16)]
    return pl.pallas_call(
        functools.partial(_front_kernel, nb=nb),
        grid=(g, r // tr),
        in_specs=[tok(d), tail(rt_m, mw2), tail(rt_r, R_COLS), const((1, d)),
                  const((d, 2 * mw2)), const((d, R_COLS)), const((d, 2 * d)), const((d, LANES)),
                  const((SUBLANES, mw2)), const((1, mw2)), const((1, mw2)), const((1, R_COLS)),
                  const((LANES, w)), const((LANES, w)), const((LANES, w)),
                  const((1, w)), const((1, w)), const((1, w)), const((1, w)), const((1, w))],
        out_specs=[tok(c) for c, _ in out_cols] + [tail(rt_m, mw2), tail(rt_r, R_COLS)],
        out_shape=[jax.ShapeDtypeStruct((g, r, c), dt) for c, dt in out_cols]
        + [jax.ShapeDtypeStruct((g, rt_m, mw2), F32), jax.ShapeDtypeStruct((g, rt_r, R_COLS), F32)],
        scratch_shapes=[pltpu.VMEM((rt_m, mw2), F32), pltpu.VMEM((rt_r, R_COLS), F32)],
        compiler_params=_cparams("parallel", "arbitrary"),
        name=name,
    )(x, mstate, rstate, wts["norm1_g"], wts["w_m"], wts["w_r"], wts["w_g"], wts["w_if"], wts["m_conv_w"], wts["m_conv_b"], wts["m_qk_scale"],
      wts["r_mix"], wts["r_w2"], wts["r_a2"], wts["r_g2"], wts["r_w0"], wts["r_a0"], wts["r_kk"], wts["r_ka"],
      wts["r_rk"])


def _mlstm_chunks(chunks, ng_ref, lb, lp):
    hd, nh = M_HEAD_DIM, M_HEADS
    row = lax.broadcasted_iota(jnp.int32, (lp, LANES), 0)
    lane = lax.broadcasted_iota(jnp.int32, (lp, LANES), 1)
    valid = row < lb
    is_forget = valid & (lane >= nh) & (lane < 2 * nh)
    sq_r = lax.broadcasted_iota(jnp.int32, (lp, lp), 0)
    sq_c = lax.broadcasted_iota(jnp.int32, (lp, lp), 1)
    tril = jnp.where(sq_r >= sq_c, 1.0, 0.0).astype(BF16)
    keep = sq_r <= sq_c

    xgs = []
    for qk, v, gates, state in chunks:
        li = jnp.where(valid, gates, -jnp.inf)
        lf = jnp.where(is_forget, _log_sigmoid(gates), 0.0)
        bcum = _split_dot_lhs01(tril, lf, 2)
        xg = jnp.where(lane < nh, li, bcum)
        xgs.append((xg, xg.T))

    items = []
    for (qk, v, gates, (ct, n, m)), (xg, xg_t) in zip(chunks, xgs):
        for h in range(nh):
            brow = xg_t[nh + h:nh + h + 1, :]
            irow = xg_t[h:h + 1, :]
            ccol = xg[:, h:h + 1] - xg[:, nh + h:nh + h + 1]
            q = qk[:, h * hd:(h + 1) * hd]
            k = qk[:, M_WIDTH + h * hd:M_WIDTH + (h + 1) * hd]
            qb, kb = q.astype(BF16), k.astype(BF16)
            v_t = v[:, h * hd:(h + 1) * hd].T
            dm = jnp.where(keep, brow + ccol, -jnp.inf)
            inter = brow + m[h]
            mt = jnp.maximum(inter, jnp.max(dm, axis=0, keepdims=True))
            s_t = lax.dot_general(kb, qb, _NT, preferred_element_type=F32)
            cq = lax.dot_general(ct[h].astype(BF16), qb, _NT, preferred_element_type=F32)
            qn = _dot_pieces(jnp.broadcast_to(n[h], (SUBLANES, hd)), q, _NT)[0:1, :]
            btot = brow[:, lp - 1:lp]
            wlog = btot - brow + irow
            m_h = jnp.maximum(btot + m[h], jnp.max(wlog, axis=1, keepdims=True))
            items.append(dict(h=h, k=k, kb=kb, v_t=v_t, dm=dm, inter=inter, mt=mt, s_t=s_t, cq=cq, qn=qn,
                              btot=btot, wlog=wlog, m_h=m_h, ct=ct[h], n=n[h], m=m[h]))

    for it in items:
        p_t = jnp.exp(it["dm"] - it["mt"]) * it["s_t"]
        sc = jnp.exp(it["inter"] - it["mt"])
        num = sc * it["cq"] + jnp.dot(it["v_t"].astype(BF16), p_t.astype(BF16), preferred_element_type=F32)
        den = sc * it["qn"] + jnp.sum(p_t, axis=0, keepdims=True)
        hh = num / jnp.maximum(jnp.abs(den), jnp.exp(-it["mt"]))
        mu = jnp.mean(hh, axis=0, keepdims=True)
        d = hh - mu
        it["out"] = d * lax.rsqrt(jnp.mean(d * d, axis=0, keepdims=True) + EPS) * ng_ref[it["h"], :, 0:lp]

    for it in items:
        s0 = jnp.exp(it["btot"] + it["m"] - it["m_h"])
        ws = jnp.exp(it["wlog"] - it["m_h"])
        it["ct_new"] = s0 * it["ct"] + jnp.dot((it["v_t"] * ws).astype(BF16), it["kb"], preferred_element_type=F32)
        it["n_new"] = s0 * it["n"] + _dot_pieces(jnp.broadcast_to(ws, (SUBLANES, lp)), it["k"], _NN)[0:1, :]

    res = []
    for c in range(len(chunks)):
        its = items[c * nh:(c + 1) * nh]
        res.append(([it["out"] for it in its],
                    ([it["ct_new"] for it in its], [it["n_new"] for it in its], [it["m_h"] for it in its])))
    return res


def _mlstm_kernel(qk_ref, v_ref, og_ref, gt_ref, gb_ref, ng_ref, c0_ref, n0_ref, m0_ref,
                  h_ref, ct_ref, nt_ref, mt_ref, c_scr, n_scr, m_scr, *, lb, nsub, bb):
    lp, hd, nh = max(lb, MLSTM_MIN_CHUNK), M_HEAD_DIM, M_HEADS

    @pl.when(pl.program_id(1) == 0)
    def _():
        for si in range(bb):
            for h in range(nh):
                c_scr[si, h] = c0_ref[si, h].T
        n_scr[...] = n0_ref[...]
        m_scr[...] = m0_ref[...]

    def pad(x):
        if lb == lp:
            return x
        return jnp.concatenate([x, jnp.zeros((lp - lb, x.shape[1]), x.dtype)], axis=0)

    states = []
    for si in range(bb):
        n_all, m_all = n_scr[si], m_scr[si]
        states.append(([c_scr[si, h] for h in range(nh)], [n_all[h:h + 1, :] for h in range(nh)],
                       [m_all[h:h + 1, 0:1] for h in range(nh)]))
    for ci in range(nsub):
        rows = slice(ci * lb, (ci + 1) * lb)
        res = _mlstm_chunks([(pad(qk_ref[si, rows, :]), pad(v_ref[si, rows, :]),
                              pad(gt_ref[si, rows, :]) + gb_ref[...], states[si]) for si in range(bb)],
                            ng_ref, lb, lp)
        for si, (outs, state) in enumerate(res):
            states[si] = state
            for h in range(nh):
                cols = slice(h * hd, (h + 1) * hd)
                h_ref[si, rows, cols] = _sigmoid(og_ref[si, rows, cols]) * outs[h].T[:lb]
    for si, (ct, n, m) in enumerate(states):
        for h in range(nh):
            c_scr[si, h] = ct[h]
        n_scr[si, 0:nh, :] = jnp.concatenate(n, axis=0)
        m_scr[si, 0:nh, :] = jnp.concatenate([jnp.broadcast_to(x, (1, LANES)) for x in m], axis=0)

    @pl.when(pl.program_id(1) == pl.num_programs(1) - 1)
    def _():
        for si in range(bb):
            for h in range(nh):
                ct_ref[si, h] = c_scr[si, h].T
        nt_ref[...] = n_scr[...]
        mt_ref[...] = m_scr[...]


def _mlstm(qk, v, og, gt, gbias, ng, c0, n0, m0, name):
    b, t, _ = qk.shape
    lb = min(t, MLSTM_CHUNK)
    nsub = min(t // lb, MLSTM_SUBCHUNKS)
    bb = 1 if nsub > 1 else min(b, MLSTM_SHORT_SEQS)
    assert t % (lb * nsub) == 0 and b % bb == 0
    tok = lambda w: pl.BlockSpec((bb, lb * nsub, w), lambda a, c: (a, c, 0))
    const = lambda shp: pl.BlockSpec(shp, lambda a, c: tuple(0 for _ in shp))
    st4 = pl.BlockSpec((bb, M_HEADS, M_HEAD_DIM, M_HEAD_DIM), lambda a, c: (a, 0, 0, 0))
    st3 = pl.BlockSpec((bb, SUBLANES, LANES), lambda a, c: (a, 0, 0))
    return pl.pallas_call(
        functools.partial(_mlstm_kernel, lb=lb, nsub=nsub, bb=bb),
        grid=(b // bb, t // (lb * nsub)),
        in_specs=[tok(2 * M_WIDTH), tok(M_WIDTH), tok(M_WIDTH), tok(LANES),
                  const((1, LANES)), const((M_HEADS, M_HEAD_DIM, MLSTM_CHUNK)), st4, st3, st3],
        out_specs=[tok(M_WIDTH), st4, st3, st3],
        out_shape=[jax.ShapeDtypeStruct((b, t, M_WIDTH), F32),
                   jax.ShapeDtypeStruct((b, M_HEADS, M_HEAD_DIM, M_HEAD_DIM), F32),
                   jax.ShapeDtypeStruct((b, SUBLANES, LANES), F32),
                   jax.ShapeDtypeStruct((b, SUBLANES, LANES), F32)],
        scratch_shapes=[pltpu.VMEM((bb, M_HEADS, M_HEAD_DIM, M_HEAD_DIM), F32),
                        pltpu.VMEM((bb, SUBLANES, LANES), F32),
                        pltpu.VMEM((bb, SUBLANES, LANES), F32)],
        compiler_params=_cparams("parallel", "arbitrary"),
        name=name,
    )(qk, v, og, gt, gbias, ng, c0, n0, m0)


def _rscan_kernel(r_ref, w_ref, k_ref, v_ref, a_ref, b_ref, s0_ref, y_ref, st_ref, s_scr, y_scr, *, bg, steps):
    hp = R_HEADS // 2
    npair = bg * hp
    nquad = npair // 2
    hd = R_HEAD_DIM
    tc = pl.program_id(1)
    pairs = [(bi, p) for bi in range(bg) for p in range(hp)]
    sl = lambda q: slice(q * hd, (q + 1) * hd)

    @pl.when(tc == 0)
    def _():
        for q, (bi, p) in enumerate(pairs):
            s_scr[sl(q), :] = jnp.concatenate([s0_ref[bi, 2 * p], s0_ref[bi, 2 * p + 1]], axis=1)

    y_scr[...] = jnp.zeros(y_scr.shape, F32)
    ones = _head_ones(2 * LANES)
    lane = lax.broadcasted_iota(jnp.int32, (hd, LANES), 1) & (hd - 1)
    diag = lane == lax.broadcasted_iota(jnp.int32, (hd, LANES), 0)

    def side_by_side(xs):
        return jnp.concatenate([jnp.concatenate([xs[2 * u], xs[2 * u + 1]], axis=1) for u in range(nquad)], axis=0)

    def pair_of(res, q):
        return res[(q // 2) * hd:(q // 2 + 1) * hd, (q % 2) * LANES:(q % 2 + 1) * LANES]

    def step8(t8, carry):
        tiles = [[ref[bi, t8, :, p * LANES:(p + 1) * LANES] for bi, p in pairs]
                 for ref in (r_ref, w_ref, k_ref, v_ref, a_ref, b_ref)]
        s_cur = [s_scr[sl(q), :] for q in range(npair)]
        for j in range(SUBLANES):
            r_t, w_t, k_t, v_t, a_t, b_t = [[tl[q][j:j + 1, :] for q in range(npair)] for tl in tiles]
            prod = side_by_side([s_cur[q] * a_t[q] for q in range(npair)])
            sa = _split_dot(prod, ones, SCAN_SA_PIECES)
            vdiag = side_by_side([jnp.where(diag, v_t[q], 0.0) for q in range(npair)])
            vcol = jnp.dot(vdiag.astype(BF16), ones, preferred_element_type=F32)
            s_cur = [s_cur[q] * w_t[q] + pair_of(sa, q) * b_t[q] + pair_of(vcol, q) * k_t[q] for q in range(npair)]
            outs = side_by_side([s_cur[q] * r_t[q] for q in range(npair)])
            yb = jnp.dot(outs.astype(BF16), ones, preferred_element_type=F32)
            hit = lane == t8 * SUBLANES + j
            for q in range(npair):
                pltpu.store(y_scr.at[sl(q), :], pair_of(yb, q), mask=hit)
        for q in range(npair):
            s_scr[sl(q), :] = s_cur[q]
        return carry

    lax.fori_loop(0, steps // SUBLANES, step8, 0)

    low = lax.broadcasted_iota(jnp.int32, (hd, LANES), 1) < hd
    for u in range(nquad):
        yt = y_scr[u * LANES:(u + 1) * LANES, :].T
        top, bot = yt[:hd], yt[hd:]
        bi, p0 = pairs[2 * u]
        y_ref[bi, :, p0 * LANES:(p0 + 1) * LANES] = jnp.where(low, top, pltpu.roll(bot, hd, 1))[:steps]
        y_ref[bi, :, (p0 + 1) * LANES:(p0 + 2) * LANES] = jnp.where(low, pltpu.roll(top, hd, 1), bot)[:steps]

    @pl.when(tc == pl.num_programs(1) - 1)
    def _():
        for q, (bi, p) in enumerate(pairs):
            s_fin = s_scr[sl(q), :]
            st_ref[bi, 2 * p] = s_fin[:, :hd]
            st_ref[bi, 2 * p + 1] = s_fin[:, hd:]


def _rscan(ops, s0, name):
    nbatch, nsteps, _ = ops[0].shape
    bg = SCAN_BATCH_GROUP
    steps = min(nsteps, SCAN_CHUNK)
    assert nbatch % bg == 0 and nsteps % steps == 0 and steps % SUBLANES == 0
    rows = bg * (R_HEADS // 2) * R_HEAD_DIM
    ops = [o.reshape(nbatch, nsteps // SUBLANES, SUBLANES, R_WIDTH) for o in ops]
    opspec = pl.BlockSpec((bg, steps // SUBLANES, SUBLANES, R_WIDTH), lambda g, c: (g, c, 0, 0))
    stspec = pl.BlockSpec((bg, R_HEADS, R_HEAD_DIM, R_HEAD_DIM), lambda g, c: (g, 0, 0, 0))
    return pl.pallas_call(
        functools.partial(_rscan_kernel, bg=bg, steps=steps),
        grid=(nbatch // bg, nsteps // steps),
        in_specs=[opspec] * 6 + [stspec],
        out_specs=[pl.BlockSpec((bg, steps, R_WIDTH), lambda g, c: (g, c, 0)), stspec],
        out_shape=[jax.ShapeDtypeStruct((nbatch, nsteps, R_WIDTH), F32),
                   jax.ShapeDtypeStruct(s0.shape, F32)],
        scratch_shapes=[pltpu.VMEM((rows, LANES), F32), pltpu.VMEM((rows, LANES), F32)],
        compiler_params=_cparams("parallel", "arbitrary"),
        name=name,
    )(*ops, s0)


def _merge_kernel(x_ref, hm_ref, y_ref, bonus_ref, g_ref, zg_ref, lng_ref, lnb_ref,
                  wbm_ref, wbr_ref, wo_ref, o_ref):
    ones = _head_ones(2 * LANES)
    y = y_ref[...]
    inv = 1.0 / R_HEAD_DIM
    mu = _head_sum64(y, ones) * inv
    d = y - mu
    var = _head_sum64(d * d, ones) * inv
    yn = d * lax.rsqrt(var + GN_EPS) * lng_ref[...] + lnb_ref[...]
    yr = (yn + bonus_ref[...].astype(F32)) * g_ref[...].astype(F32)
    y_b = _bdot(yr, wbr_ref[...])
    y_a = _bdot(hm_ref[...], wbm_ref[...])
    dm = x_ref.shape[1]
    merged = _sigmoid(zg_ref[:, 0:dm].astype(F32)) * y_a + _sigmoid(zg_ref[:, dm:2 * dm].astype(F32)) * y_b
    o_ref[...] = x_ref[...] + _bdot(merged, wo_ref[...])


def _merge(x, hm, y, bonus, g, zg, lng, lnb, wbm, wbr, wo, tm, name):
    n, d = x.shape
    w = R_WIDTH
    tok = lambda c: pl.BlockSpec((tm, c), lambda i: (i, 0))
    const = lambda shp: pl.BlockSpec(shp, lambda i: tuple(0 for _ in shp))
    return pl.pallas_call(
        _merge_kernel,
        grid=(n // tm,),
        in_specs=[tok(d), tok(M_WIDTH), tok(w), tok(w), tok(w), tok(2 * d),
                  const((1, w)), const((1, w)),
                  const((M_WIDTH, d)), const((w, d)), const((d, d))],
        out_specs=tok(d),
        out_shape=jax.ShapeDtypeStruct((n, d), F32),
        compiler_params=_cparams("parallel"),
        name=name,
    )(x, hm, y, bonus, g, zg, lng, lnb, wbm, wbr, wo)


def _ffn_kernel(x_ref, pe_ref, st_ref, n2g_ref, wup_ref, cw_ref, cb_ref, fd_ref, png_ref, pgw_ref,
                ppj_ref, fng_ref, y_ref, tail_ref, tail_scr, *, nb, rt, dff):
    @pl.when(pl.program_id(1) == 0)
    def _():
        tail_scr[...] = st_ref[0]

    x1 = x_ref[0]
    h2 = _rmsnorm(x1, n2g_ref[...]).astype(BF16)
    uc = jnp.dot(h2, wup_ref[:, 0:dff], preferred_element_type=F32)
    tail = tail_scr[...]
    c = cb_ref[...] + cw_ref[F_CONV - 1:F_CONV, :] * uc
    for j in range(F_CONV - 1):
        c = c + cw_ref[j:j + 1, :] * _shift_rows(uc, tail, (F_CONV - 1 - j) * nb)
    tail_scr[...] = uc[uc.shape[0] - rt:]
    ug = jnp.dot(h2, wup_ref[:, dff:2 * dff], preferred_element_type=F32)
    x2 = x1 + _bdot(_gelu_tanh(c) * ug, fd_ref[...])
    gate = _sigmoid(_bdot(_rmsnorm(x2, png_ref[...]), pgw_ref[...]))
    x3 = x2 + gate * _bdot(pe_ref[0], ppj_ref[...])
    y_ref[0] = _rmsnorm(x3, fng_ref[...])

    @pl.when(pl.program_id(1) == pl.num_programs(1) - 1)
    def _():
        tail_ref[0] = tail_scr[...]


def _ffn(x, pe, state, n2g, wup, cw8, cb, fd, png, pgw, ppj, fng, nb, tr, name):
    g, r, d = x.shape
    dff = fd.shape[0]
    pdim = pe.shape[2]
    rt = state.shape[1]
    assert r % tr == 0 and tr >= rt
    tok = lambda c: pl.BlockSpec((1, tr, c), lambda a, i: (a, i, 0))
    const = lambda shp: pl.BlockSpec(shp, lambda a, i: tuple(0 for _ in shp), pipeline_mode=pl.Buffered(1))
    tailspec = pl.BlockSpec((1, rt, dff), lambda a, i: (a, 0, 0))
    return pl.pallas_call(
        functools.partial(_ffn_kernel, nb=nb, rt=rt, dff=dff),
        grid=(g, r // tr),
        in_specs=[tok(d), tok(pdim), tailspec, const((1, d)), const((d, 2 * dff)),
                  const((SUBLANES, dff)), const((1, dff)), const((dff, d)), const((1, d)), const((d, d)),
                  const((pdim, d)), const((1, d))],
        out_specs=[tok(d), tailspec],
        out_shape=[jax.ShapeDtypeStruct((g, r, d), F32), jax.ShapeDtypeStruct((g, rt, dff), F32)],
        scratch_shapes=[pltpu.VMEM((rt, dff), F32)],
        compiler_params=_cparams("parallel", "arbitrary"),
        name=name,
    )(x, pe, state, n2g, wup, cw8, cb, fd, png, pgw, ppj, fng)


def _pad_rows(w, rows):
    return jnp.pad(w, ((0, rows - w.shape[0]), (0, 0)))


def _layer(x, pe, st, wts, nb):
    g, r, d = x.shape
    n = g * r
    nseq = g * nb
    nsteps = r // nb
    xf = x.reshape(n, d)
    flat = lambda a: a.reshape(n, -1)

    tr = FRONT_ROWS
    (qk, zv, zo, zg, zif, rr, rw, rk, rv, ra, rb, rg, bonus, mtail, rtail) = _front(
        x, st["mconv"], st["rshift"], wts, nb, tr, "front")

    to_seq = (lambda a: a.reshape(nseq, nsteps, a.shape[-1])) if nb == 1 else \
        (lambda a: a.reshape(nsteps, nseq, a.shape[-1]).transpose(1, 0, 2))
    from_seq = (lambda a: a.reshape(n, -1)) if nb == 1 else (lambda a: a.transpose(1, 0, 2).reshape(n, -1))

    hm, mc, mn, mm = _mlstm(to_seq(flat(qk)), to_seq(flat(zv)), to_seq(flat(zo)), to_seq(flat(zif)),
                            wts["m_gate_bias"], wts["m_norm_g"], st["mC"], st["mn"], st["mm"], "mlstm")

    y, s_new = _rscan([to_seq(flat(o)) for o in (rr, rw, rk, rv, ra, rb)], st["rS"], "rscan")

    x1 = _merge(xf, from_seq(hm), from_seq(y), flat(bonus), flat(rg), flat(zg), wts["r_ln_g"], wts["r_ln_b"],
                wts["w_branch_m"], wts["w_branch_r"], wts["w_out"], MERGE_ROWS, "merge")

    tr_f = max(FFN_ROWS, st["fconv"].shape[1])
    yout, utail = _ffn(x1.reshape(g, r, d), pe, st["fconv"], wts["norm2_g"], wts["f_up"], wts["f_conv_w"],
                       wts["f_conv_b"], wts["f_down"], wts["ple_norm_g"], wts["ple_gate_w"], wts["ple_proj"],
                       wts["final_norm_g"], nb, tr_f, "ffn")
    return yout, dict(mtail=mtail, rtail=rtail, utail=utail, mC=mc, mn=mn, mm=mm, rS=s_new)


def kernel(x_prompt, x_sample, state_mlstm_conv, state_mlstm_C, state_mlstm_n, state_mlstm_m,
           state_rwkv_shift, state_rwkv_S, state_ffn_conv, p_prompt, p_sample,
           norm1_g, w_in, m_conv_w, m_conv_b, m_i_bias, m_f_bias, m_norm_g, w_branch_m,
           r_mix, r_w0, r_w2, r_a0, r_a2, r_g2, r_kk, r_ka, r_rk, r_ln_g, r_ln_b, w_branch_r,
           w_out, norm2_g, f_up, f_conv_w, f_conv_b, f_down, ple_norm_g, ple_gate_w, ple_proj,
           final_norm_g):
    assert norm1_g.shape[0] == 1, "single trunk layer"
    bp, tp, d = x_prompt.shape
    bs, ts, _ = x_sample.shape
    dff = f_down.shape[1]
    mw2 = 2 * M_WIDTH
    row = lambda a: a.reshape(1, -1).astype(F32)

    wi = w_in[0]
    c_if = 4 * M_WIDTH
    c_r = c_if + 2 * M_HEADS
    c_g = c_r + R_COLS
    lora = jnp.zeros((LANES, R_WIDTH), F32)
    wts = dict(
        norm1_g=row(norm1_g[0]),
        w_m=wi[:, :c_if].astype(BF16), w_r=wi[:, c_r:c_g].astype(BF16), w_g=wi[:, c_g:].astype(BF16),
        w_if=jnp.pad(wi[:, c_if:c_r], ((0, 0), (0, LANES - 2 * M_HEADS))).astype(BF16),
        m_conv_w=_pad_rows(m_conv_w[0], SUBLANES), m_conv_b=row(m_conv_b[0]),
        m_qk_scale=jnp.concatenate([jnp.ones((1, M_WIDTH), F32),
                                    jnp.full((1, M_WIDTH), M_HEAD_DIM ** -0.5, F32)], axis=1),
        m_gate_bias=jnp.pad(jnp.concatenate([m_i_bias[0], m_f_bias[0]]).reshape(1, -1),
                            ((0, 0), (0, LANES - 2 * M_HEADS))),
        m_norm_g=jnp.broadcast_to(m_norm_g[0].reshape(M_HEADS, M_HEAD_DIM, 1), (M_HEADS, M_HEAD_DIM, MLSTM_CHUNK)),
        w_branch_m=w_branch_m[0].astype(BF16),
        r_mix=row(r_mix[0]), r_w0=row(r_w0[0]), r_a0=row(r_a0[0]),
        r_w2=lora.at[:R_DECAY_LORA].set(r_w2[0]).astype(BF16),
        r_a2=lora.at[R_DECAY_LORA:R_DECAY_LORA + R_A_LORA].set(r_a2[0]).astype(BF16),
        r_g2=r_g2[0].astype(BF16), r_kk=row(r_kk[0]), r_ka=row(r_ka[0]), r_rk=row(r_rk[0]),
        r_ln_g=row(r_ln_g[0]), r_ln_b=row(r_ln_b[0]), w_branch_r=w_branch_r[0].astype(BF16),
        w_out=w_out[0].astype(BF16), norm2_g=row(norm2_g[0]), f_up=f_up[0].astype(BF16),
        f_conv_w=_pad_rows(f_conv_w[0], SUBLANES), f_conv_b=row(f_conv_b[0]),
        f_down=f_down[0].astype(BF16), ple_norm_g=row(ple_norm_g[0]), ple_gate_w=ple_gate_w[0].astype(BF16),
        ple_proj=ple_proj[0].astype(BF16), final_norm_g=row(final_norm_g),
    )

    def seq_states(c, nv, m):
        b = c.shape[0]
        nv8 = jnp.pad(nv, ((0, 0), (0, SUBLANES - M_HEADS), (0, 0)))
        m8 = jnp.pad(jnp.broadcast_to(m[:, :, None], (b, M_HEADS, LANES)), ((0, 0), (0, SUBLANES - M_HEADS), (0, 0)))
        return c, nv8, m8

    zc, zn, zm = seq_states(jnp.zeros((bp, M_HEADS, M_HEAD_DIM, M_HEAD_DIM), F32),
                            jnp.zeros((bp, M_HEADS, M_HEAD_DIM), F32), jnp.zeros((bp, M_HEADS), F32))
    st_p = dict(mconv=jnp.zeros((bp, SUBLANES, mw2), F32), rshift=jnp.zeros((bp, SUBLANES, R_COLS), F32),
                fconv=jnp.zeros((bp, SUBLANES, dff), F32), mC=zc, mn=zn, mm=zm,
                rS=jnp.zeros((bp, R_HEADS, R_HEAD_DIM, R_HEAD_DIM), F32))
    y_p, new_p = _layer(x_prompt, p_prompt[0], st_p, wts, 1)

    tmaj = lambda a: a.transpose(1, 0, 2).reshape(1, a.shape[0] * a.shape[1], a.shape[2])
    sc, sn, sm = seq_states(state_mlstm_C[0], state_mlstm_n[0], state_mlstm_m[0])
    st_s = dict(mconv=tmaj(state_mlstm_conv[0]), rshift=state_rwkv_shift[0][None], fconv=tmaj(state_ffn_conv[0]),
                mC=sc, mn=sn, mm=sm, rS=state_rwkv_S[0])
    y_s, new_s = _layer(tmaj(x_sample), tmaj(p_sample[0]), st_s, wts, bs)
    y_sample = y_s.reshape(ts, bs, d).transpose(1, 0, 2)

    def finish(new, b, tmajor):
        def last(tail, k):
            if tmajor:
                return tail.reshape(-1, b, tail.shape[-1])[-k:].transpose(1, 0, 2)
            return tail[:, tail.shape[1] - k:]
        return (last(new["mtail"], M_CONV - 1)[None], new["mC"][None], new["mn"][:, :M_HEADS][None],
                new["mm"][:, :M_HEADS, 0][None], last(new["rtail"], 1)[:, 0][None],
                new["rS"][None], last(new["utail"], F_CONV - 1)[None])

    return (y_p, y_sample) + finish(new_p, bp, False) + finish(new_s, bs, True)
```

```python
import functools

import jax
import jax.numpy as jnp
from jax import lax
from jax.experimental import pallas as pl
from jax.experimental.pallas import tpu as pltpu

F32 = jnp.float32
BF16 = jnp.bfloat16

M_HEADS = 4
M_HEAD_DIM = 128
M_WIDTH = M_HEADS * M_HEAD_DIM
M_CONV = 4
R_HEADS = 8
R_HEAD_DIM = 64
R_WIDTH = R_HEADS * R_HEAD_DIM
R_DECAY_LORA = 64
R_A_LORA = 64
R_GATE_LORA = 128
R_COLS = 3 * R_WIDTH + R_DECAY_LORA + R_A_LORA + R_GATE_LORA
F_CONV = 3
EPS = 1e-6
GN_EPS = 64e-5

LANES = 128
SUBLANES = 8
VMEM_LIMIT_BYTES = 56 * 1024 * 1024

FRONT_ROWS = 512
FFN_ROWS = 512
MLSTM_CHUNK = 256
MLSTM_MIN_CHUNK = 128
MLSTM_SUBCHUNKS = 8
MLSTM_SHORT_SEQS = 8
SCAN_CHUNK = 64
SCAN_BATCH_GROUP = 8
SCAN_SA_PIECES = 1


def _cparams(*sem):
    return pltpu.CompilerParams(dimension_semantics=sem, vmem_limit_bytes=VMEM_LIMIT_BYTES)


def _sigmoid(x):
    return 1.0 / (1.0 + jnp.exp(-x))


def _softplus(x):
    return jnp.maximum(x, 0.0) + jnp.log1p(jnp.exp(-jnp.abs(x)))


def _log_sigmoid(x):
    return -_softplus(-x)


def _gelu_tanh(x):
    return 0.5 * x * (1.0 + jnp.tanh(0.7978845608028654 * (x + 0.044715 * (x * x * x))))


def _bdot(a, b):
    return jnp.dot(a.astype(BF16), b.astype(BF16), preferred_element_type=F32)


def _rmsnorm(x, g):
    return x * lax.rsqrt(jnp.mean(x * x, axis=-1, keepdims=True) + EPS) * g


def _split_dot(x, m01, parts):
    acc = None
    r = x
    for p in range(parts):
        h = r.astype(BF16)
        d = jnp.dot(h, m01, preferred_element_type=F32)
        acc = d if acc is None else acc + d
        if p + 1 < parts:
            r = r - h.astype(F32)
    return acc


def _split_dot_lhs01(m01, x, parts):
    acc = None
    r = x
    for p in range(parts):
        h = r.astype(BF16)
        d = jnp.dot(m01, h, preferred_element_type=F32)
        acc = d if acc is None else acc + d
        if p + 1 < parts:
            r = r - h.astype(F32)
    return acc


_NN = (((1,), (0,)), ((), ()))
_NT = (((1,), (1,)), ((), ()))


def _dot_pieces(a, b, dims):
    a_hi, b_hi = a.astype(BF16), b.astype(BF16)
    a_lo, b_lo = (a - a_hi.astype(F32)).astype(BF16), (b - b_hi.astype(F32)).astype(BF16)
    dot = lambda x, y: lax.dot_general(x, y, dims, preferred_element_type=F32)
    return dot(a_hi, b_hi) + dot(a_hi, b_lo) + dot(a_lo, b_hi)


def _head_ones(n):
    r = lax.broadcasted_iota(jnp.int32, (n, n), 0) // R_HEAD_DIM
    c = lax.broadcasted_iota(jnp.int32, (n, n), 1) // R_HEAD_DIM
    return jnp.where(r == c, 1.0, 0.0).astype(BF16)


def _head_sum64(x, ones):
    wd = ones.shape[0]
    outs = [_split_dot(x[:, s:s + wd], ones, 2) for s in range(0, x.shape[1], wd)]
    return outs[0] if len(outs) == 1 else jnp.concatenate(outs, axis=1)


def _shift_rows(x, tail, s):
    n, rt = x.shape[0], tail.shape[0]
    if s % SUBLANES == 0:
        if s == n:
            return tail[rt - s:]
        return jnp.concatenate([tail[rt - s:], x[:n - s]], axis=0)
    assert s < SUBLANES
    r = pltpu.roll(x, s, 0)
    t8 = pltpu.roll(tail[rt - SUBLANES:], s, 0)
    row = lax.broadcasted_iota(jnp.int32, (SUBLANES, x.shape[1]), 0)
    first = jnp.where(row < s, t8, r[:SUBLANES])
    if n == SUBLANES:
        return first
    return jnp.concatenate([first, r[SUBLANES:]], axis=0)


def _front_kernel(x_ref, mst_ref, rst_ref, n1g_ref, wm_ref, wr_ref, wg_ref, wif_ref, cw_ref, cb_ref, qsc_ref, mix_ref, w2_ref, a2_ref,
                  g2_ref, w0_ref, a0_ref, kk_ref, ka_ref, rk_ref,
                  qk_o, zv_o, zo_o, zg_o, zif_o, r_o, w_o, k_o, v_o, a_o, b_o, g_o, bonus_o, mtail_o, rtail_o,
                  mtail_scr, rtail_scr, *, nb):
    @pl.when(pl.program_id(1) == 0)
    def _():
        mtail_scr[...] = mst_ref[0]
        rtail_scr[...] = rst_ref[0]

    h = _rmsnorm(x_ref[0], n1g_ref[...]).astype(BF16)
    proj = lambda w: jnp.dot(h, w, preferred_element_type=F32)
    tr = h.shape[0]
    mw2 = 2 * M_WIDTH

    z = proj(wr_ref[...])
    rtail = rtail_scr[...]
    prev = _shift_rows(z, rtail, nb)
    rtail_scr[...] = z[tr - rtail.shape[0]:]
    xm = z + (prev - z) * mix_ref[...]
    w = R_WIDTH
    r, kr, vr = xm[:, 0:w], xm[:, w:2 * w], xm[:, 2 * w:3 * w]
    lora_in = xm[:, 3 * w:3 * w + LANES]
    gd = xm[:, 3 * w + LANES:3 * w + 2 * LANES]
    lw = _bdot(jnp.tanh(lora_in), w2_ref[...])
    la = _bdot(lora_in, a2_ref[...])
    wlog = -_softplus(-(w0_ref[...] + lw)) - 0.5
    a = _sigmoid(a0_ref[...] + la)
    ones = _head_ones(2 * LANES)
    kk = kr * kk_ref[...]
    kk = kk / jnp.maximum(jnp.sqrt(_head_sum64(kk * kk, ones)), 1e-12)
    k2 = kr * (1.0 + (a - 1.0) * ka_ref[...])
    r_o[0] = r
    w_o[0] = jnp.exp(-jnp.exp(wlog))
    k_o[0] = k2
    v_o[0] = vr
    a_o[0] = -kk
    b_o[0] = kk * a
    g_o[0] = _bdot(_sigmoid(gd), g2_ref[...]).astype(g_o.dtype)
    bonus_o[0] = (_head_sum64(r * k2 * rk_ref[...], ones) * vr).astype(bonus_o.dtype)

    zqk = proj(wm_ref[:, 0:mw2])
    mtail = mtail_scr[...]
    y = cb_ref[...] + cw_ref[M_CONV - 1:M_CONV, :] * zqk
    for j in range(M_CONV - 1):
        y = y + cw_ref[j:j + 1, :] * _shift_rows(zqk, mtail, (M_CONV - 1 - j) * nb)
    mtail_scr[...] = zqk[tr - mtail.shape[0]:]
    qk_o[0] = y * _sigmoid(y) * qsc_ref[...]
    zv_o[0] = proj(wm_ref[:, mw2:mw2 + M_WIDTH])
    zo_o[0] = proj(wm_ref[:, mw2 + M_WIDTH:mw2 + 2 * M_WIDTH])
    zg_o[0] = proj(wg_ref[...]).astype(zg_o.dtype)
    zif_o[0] = proj(wif_ref[...])

    @pl.when(pl.program_id(1) == pl.num_programs(1) - 1)
    def _():
        mtail_o[0] = mtail_scr[...]
        rtail_o[0] = rtail_scr[...]


def _front(x, mstate, rstate, wts, nb, tr, name):
    g, r, d = x.shape
    rt_m, rt_r = mstate.shape[1], rstate.shape[1]
    assert r % tr == 0 and tr >= max(rt_m, rt_r)
    mw2, w = 2 * M_WIDTH, R_WIDTH
    tok = lambda c: pl.BlockSpec((1, tr, c), lambda a, i: (a, i, 0))
    const = lambda shp: pl.BlockSpec(shp, lambda a, i: tuple(0 for _ in shp), pipeline_mode=pl.Buffered(1))
    tail = lambda rows, c: pl.BlockSpec((1, rows, c), lambda a, i: (a, 0, 0))
    out_cols = [(mw2, F32), (M_WIDTH, F32), (M_WIDTH, F32), (2 * d, BF16), (LANES, F32)] \
        + [(w, F32)] * 6 + [(w, BF16), (w, BF16)]
    return pl.pallas_call(
        functools.partial(_front_kernel, nb=nb),
        grid=(g, r // tr),
        in_specs=[tok(d), tail(rt_m, mw2), tail(rt_r, R_COLS), const((1, d)),
                  const((d, 2 * mw2)), const((d, R_COLS)), const((d, 2 * d)), const((d, LANES)),
                  const((SUBLANES, mw2)), const((1, mw2)), const((1, mw2)), const((1, R_COLS)),
                  const((LANES, w)), const((LANES, w)), const((LANES, w)),
                  const((1, w)), const((1, w)), const((1, w)), const((1, w)), const((1, w))],
        out_specs=[tok(c) for c, _ in out_cols] + [tail(rt_m, mw2), tail(rt_r, R_COLS)],
        out_shape=[jax.ShapeDtypeStruct((g, r, c), dt) for c, dt in out_cols]
        + [jax.ShapeDtypeStruct((g, rt_m, mw2), F32), jax.ShapeDtypeStruct((g, rt_r, R_COLS), F32)],
        scratch_shapes=[pltpu.VMEM((rt_m, mw2), F32), pltpu.VMEM((rt_r, R_COLS), F32)],
        compiler_params=_cparams("parallel", "arbitrary"),
        name=name,
    )(x, mstate, rstate, wts["norm1_g"], wts["w_m"], wts["w_r"], wts["w_g"], wts["w_if"], wts["m_conv_w"], wts["m_conv_b"], wts["m_qk_scale"],
      wts["r_mix"], wts["r_w2"], wts["r_a2"], wts["r_g2"], wts["r_w0"], wts["r_a0"], wts["r_kk"], wts["r_ka"],
      wts["r_rk"])


def _mlstm_chunks(chunks, ng_ref, lb, lp):
    hd, nh = M_HEAD_DIM, M_HEADS
    row = lax.broadcasted_iota(jnp.int32, (lp, LANES), 0)
    lane = lax.broadcasted_iota(jnp.int32, (lp, LANES), 1)
    valid = row < lb
    is_forget = valid & (lane >= nh) & (lane < 2 * nh)
    sq_r = lax.broadcasted_iota(jnp.int32, (lp, lp), 0)
    sq_c = lax.broadcasted_iota(jnp.int32, (lp, lp), 1)
    tril = jnp.where(sq_r >= sq_c, 1.0, 0.0).astype(BF16)
    keep = sq_r <= sq_c

    xgs = []
    for qk, v, gates, state in chunks:
        li = jnp.where(valid, gates, -jnp.inf)
        lf = jnp.where(is_forget, _log_sigmoid(gates), 0.0)
        bcum = _split_dot_lhs01(tril, lf, 2)
        xg = jnp.where(lane < nh, li, bcum)
        xgs.append((xg, xg.T))

    items = []
    for (qk, v, gates, (ct, n, m)), (xg, xg_t) in zip(chunks, xgs):
        for h in range(nh):
            brow = xg_t[nh + h:nh + h + 1, :]
            irow = xg_t[h:h + 1, :]
            ccol = xg[:, h:h + 1] - xg[:, nh + h:nh + h + 1]
            q = qk[:, h * hd:(h + 1) * hd]
            k = qk[:, M_WIDTH + h * hd:M_WIDTH + (h + 1) * hd]
            qb, kb = q.astype(BF16), k.astype(BF16)
            v_t = v[:, h * hd:(h + 1) * hd].T
            dm = jnp.where(keep, brow + ccol, -jnp.inf)
            inter = brow + m[h]
            mt = jnp.maximum(inter, jnp.max(dm, axis=0, keepdims=True))
            s_t = lax.dot_general(kb, qb, _NT, preferred_element_type=F32)
            cq = lax.dot_general(ct[h].astype(BF16), qb, _NT, preferred_element_type=F32)
            qn = _dot_pieces(jnp.broadcast_to(n[h], (SUBLANES, hd)), q, _NT)[0:1, :]
            btot = brow[:, lp - 1:lp]
            wlog = btot - brow + irow
            m_h = jnp.maximum(btot + m[h], jnp.max(wlog, axis=1, keepdims=True))
            items.append(dict(h=h, k=k, kb=kb, v_t=v_t, dm=dm, inter=inter, mt=mt, s_t=s_t, cq=cq, qn=qn,
                              btot=btot, wlog=wlog, m_h=m_h, ct=ct[h], n=n[h], m=m[h]))

    for it in items:
        p_t = jnp.exp(it["dm"] - it["mt"]) * it["s_t"]
        sc = jnp.exp(it["inter"] - it["mt"])
        num = sc * it["cq"] + jnp.dot(it["v_t"].astype(BF16), p_t.astype(BF16), preferred_element_type=F32)
        den = sc * it["qn"] + jnp.sum(p_t, axis=0, keepdims=True)
        hh = num / jnp.maximum(jnp.abs(den), jnp.exp(-it["mt"]))
        mu = jnp.mean(hh, axis=0, keepdims=True)
        d = hh - mu
        it["out"] = d * lax.rsqrt(jnp.mean(d * d, axis=0, keepdims=True) + EPS) * ng_ref[it["h"], :, 0:lp]

    for it in items:
        s0 = jnp.exp(it["btot"] + it["m"] - it["m_h"])
        ws = jnp.exp(it["wlog"] - it["m_h"])
        it["ct_new"] = s0 * it["ct"] + jnp.dot((it["v_t"] * ws).astype(BF16), it["kb"], preferred_element_type=F32)
        it["n_new"] = s0 * it["n"] + _dot_pieces(jnp.broadcast_to(ws, (SUBLANES, lp)), it["k"], _NN)[0:1, :]

    res = []
    for c in range(len(chunks)):
        its = items[c * nh:(c + 1) * nh]
        res.append(([it["out"] for it in its],
                    ([it["ct_new"] for it in its], [it["n_new"] for it in its], [it["m_h"] for it in its])))
    return res


def _mlstm_kernel(qk_ref, v_ref, og_ref, gt_ref, gb_ref, ng_ref, c0_ref, n0_ref, m0_ref,
                  h_ref, ct_ref, nt_ref, mt_ref, c_scr, n_scr, m_scr, *, lb, nsub, bb):
    lp, hd, nh = max(lb, MLSTM_MIN_CHUNK), M_HEAD_DIM, M_HEADS

    @pl.when(pl.program_id(1) == 0)
    def _():
        for si in range(bb):
            for h in range(nh):
                c_scr[si, h] = c0_ref[si, h].T
        n_scr[...] = n0_ref[...]
        m_scr[...] = m0_ref[...]

    def pad(x):
        if lb == lp:
            return x
        return jnp.concatenate([x, jnp.zeros((lp - lb, x.shape[1]), x.dtype)], axis=0)

    states = []
    for si in range(bb):
        n_all, m_all = n_scr[si], m_scr[si]
        states.append(([c_scr[si, h] for h in range(nh)], [n_all[h:h + 1, :] for h in range(nh)],
                       [m_all[h:h + 1, 0:1] for h in range(nh)]))
    for ci in range(nsub):
        rows = slice(ci * lb, (ci + 1) * lb)
        res = _mlstm_chunks([(pad(qk_ref[si, rows, :]), pad(v_ref[si, rows, :]),
                              pad(gt_ref[si, rows, :]) + gb_ref[...], states[si]) for si in range(bb)],
                            ng_ref, lb, lp)
        for si, (outs, state) in enumerate(res):
            states[si] = state
            for h in range(nh):
                cols = slice(h * hd, (h + 1) * hd)
                h_ref[si, rows, cols] = _sigmoid(og_ref[si, rows, cols]) * outs[h].T[:lb]
    for si, (ct, n, m) in enumerate(states):
        for h in range(nh):
            c_scr[si, h] = ct[h]
        n_scr[si, 0:nh, :] = jnp.concatenate(n, axis=0)
        m_scr[si, 0:nh, :] = jnp.concatenate([jnp.broadcast_to(x, (1, LANES)) for x in m], axis=0)

    @pl.when(pl.program_id(1) == pl.num_programs(1) - 1)
    def _():
        for si in range(bb):
            for h in range(nh):
                ct_ref[si, h] = c_scr[si, h].T
        nt_ref[...] = n_scr[...]
        mt_ref[...] = m_scr[...]


def _mlstm(qk, v, og, gt, gbias, ng, c0, n0, m0, name):
    b, t, _ = qk.shape
    lb = min(t, MLSTM_CHUNK)
    nsub = min(t // lb, MLSTM_SUBCHUNKS)
    bb = 1 if nsub > 1 else min(b, MLSTM_SHORT_SEQS)
    assert t % (lb * nsub) == 0 and b % bb == 0
    tok = lambda w: pl.BlockSpec((bb, lb * nsub, w), lambda a, c: (a, c, 0))
    const = lambda shp: pl.BlockSpec(shp, lambda a, c: tuple(0 for _ in shp))
    st4 = pl.BlockSpec((bb, M_HEADS, M_HEAD_DIM, M_HEAD_DIM), lambda a, c: (a, 0, 0, 0))
    st3 = pl.BlockSpec((bb, SUBLANES, LANES), lambda a, c: (a, 0, 0))
    return pl.pallas_call(
        functools.partial(_mlstm_kernel, lb=lb, nsub=nsub, bb=bb),
        grid=(b // bb, t // (lb * nsub)),
        in_specs=[tok(2 * M_WIDTH), tok(M_WIDTH), tok(M_WIDTH), tok(LANES),
                  const((1, LANES)), const((M_HEADS, M_HEAD_DIM, MLSTM_CHUNK)), st4, st3, st3],
        out_specs=[tok(M_WIDTH), st4, st3, st3],
        out_shape=[jax.ShapeDtypeStruct((b, t, M_WIDTH), F32),
                   jax.ShapeDtypeStruct((b, M_HEADS, M_HEAD_DIM, M_HEAD_DIM), F32),
                   jax.ShapeDtypeStruct((b, SUBLANES, LANES), F32),
                   jax.ShapeDtypeStruct((b, SUBLANES, LANES), F32)],
        scratch_shapes=[pltpu.VMEM((bb, M_HEADS, M_HEAD_DIM, M_HEAD_DIM), F32),
                        pltpu.VMEM((bb, SUBLANES, LANES), F32),
                        pltpu.VMEM((bb, SUBLANES, LANES), F32)],
        compiler_params=_cparams("parallel", "arbitrary"),
        name=name,
    )(qk, v, og, gt, gbias, ng, c0, n0, m0)


def _rscan_kernel(r_ref, w_ref, k_ref, v_ref, a_ref, b_ref, s0_ref, y_ref, st_ref, s_scr, y_scr, *, bg, steps):
    hp = R_HEADS // 2
    npair = bg * hp
    nquad = npair // 2
    hd = R_HEAD_DIM
    tc = pl.program_id(1)
    pairs = [(bi, p) for bi in range(bg) for p in range(hp)]
    sl = lambda q: slice(q * hd, (q + 1) * hd)

    @pl.when(tc == 0)
    def _():
        for q, (bi, p) in enumerate(pairs):
            s_scr[sl(q), :] = jnp.concatenate([s0_ref[bi, 2 * p], s0_ref[bi, 2 * p + 1]], axis=1)

    y_scr[...] = jnp.zeros(y_scr.shape, F32)
    ones = _head_ones(2 * LANES)
    lane = lax.broadcasted_iota(jnp.int32, (hd, LANES), 1) & (hd - 1)
    diag = lane == lax.broadcasted_iota(jnp.int32, (hd, LANES), 0)

    def side_by_side(xs):
        return jnp.concatenate([jnp.concatenate([xs[2 * u], xs[2 * u + 1]], axis=1) for u in range(nquad)], axis=0)

    def pair_of(res, q):
        return res[(q // 2) * hd:(q // 2 + 1) * hd, (q % 2) * LANES:(q % 2 + 1) * LANES]

    def step8(t8, carry):
        tiles = [[ref[bi, t8, :, p * LANES:(p + 1) * LANES] for bi, p in pairs]
                 for ref in (r_ref, w_ref, k_ref, v_ref, a_ref, b_ref)]
        s_cur = [s_scr[sl(q), :] for q in range(npair)]
        for j in range(SUBLANES):
            r_t, w_t, k_t, v_t, a_t, b_t = [[tl[q][j:j + 1, :] for q in range(npair)] for tl in tiles]
            prod = side_by_side([s_cur[q] * a_t[q] for q in range(npair)])
            sa = _split_dot(prod, ones, SCAN_SA_PIECES)
            vdiag = side_by_side([jnp.where(diag, v_t[q], 0.0) for q in range(npair)])
            vcol = jnp.dot(vdiag.astype(BF16), ones, preferred_element_type=F32)
            s_cur = [s_cur[q] * w_t[q] + pair_of(sa, q) * b_t[q] + pair_of(vcol, q) * k_t[q] for q in range(npair)]
            outs = side_by_side([s_cur[q] * r_t[q] for q in range(npair)])
            yb = jnp.dot(outs.astype(BF16), ones, preferred_element_type=F32)
            hit = lane == t8 * SUBLANES + j
            for q in range(npair):
                pltpu.store(y_scr.at[sl(q), :], pair_of(yb, q), mask=hit)
        for q in range(npair):
            s_scr[sl(q), :] = s_cur[q]
        return carry

    lax.fori_loop(0, steps // SUBLANES, step8, 0)

    low = lax.broadcasted_iota(jnp.int32, (hd, LANES), 1) < hd
    for u in range(nquad):
        yt = y_scr[u * LANES:(u + 1) * LANES, :].T
        top, bot = yt[:hd], yt[hd:]
        bi, p0 = pairs[2 * u]
        y_ref[bi, :, p0 * LANES:(p0 + 1) * LANES] = jnp.where(low, top, pltpu.roll(bot, hd, 1))[:steps]
        y_ref[bi, :, (p0 + 1) * LANES:(p0 + 2) * LANES] = jnp.where(low, pltpu.roll(top, hd, 1), bot)[:steps]

    @pl.when(tc == pl.num_programs(1) - 1)
    def _():
        for q, (bi, p) in enumerate(pairs):
            s_fin = s_scr[sl(q), :]
            st_ref[bi, 2 * p] = s_fin[:, :hd]
            st_ref[bi, 2 * p + 1] = s_fin[:, hd:]


def _rscan(ops, s0, name):
    nbatch, nsteps, _ = ops[0].shape
    bg = SCAN_BATCH_GROUP
    steps = min(nsteps, SCAN_CHUNK)
    assert nbatch % bg == 0 and nsteps % steps == 0 and steps % SUBLANES == 0
    rows = bg * (R_HEADS // 2) * R_HEAD_DIM
    ops = [o.reshape(nbatch, nsteps // SUBLANES, SUBLANES, R_WIDTH) for o in ops]
    opspec = pl.BlockSpec((bg, steps // SUBLANES, SUBLANES, R_WIDTH), lambda g, c: (g, c, 0, 0))
    stspec = pl.BlockSpec((bg, R_HEADS, R_HEAD_DIM, R_HEAD_DIM), lambda g, c: (g, 0, 0, 0))
    return pl.pallas_call(
        functools.partial(_rscan_kernel, bg=bg, steps=steps),
        grid=(nbatch // bg, nsteps // steps),
        in_specs=[opspec] * 6 + [stspec],
        out_specs=[pl.BlockSpec((bg, steps, R_WIDTH), lambda g, c: (g, c, 0)), stspec],
        out_shape=[jax.ShapeDtypeStruct((nbatch, nsteps, R_WIDTH), F32),
                   jax.ShapeDtypeStruct(s0.shape, F32)],
        scratch_shapes=[pltpu.VMEM((rows, LANES), F32), pltpu.VMEM((rows, LANES), F32)],
        compiler_params=_cparams("parallel", "arbitrary"),
        name=name,
    )(*ops, s0)


def _back_kernel(x_ref, hm_ref, y_ref, bonus_ref, g_ref, zg_ref, pe_ref, st_ref, lng_ref, lnb_ref, wbm_ref, wbr_ref,
                 wo_ref, n2g_ref, wup_ref, cw_ref, cb_ref, fd_ref, png_ref, pgw_ref, ppj_ref, fng_ref,
                 o_ref, tail_ref, tail_scr, *, nb, rt, dff):
    @pl.when(pl.program_id(1) == 0)
    def _():
        tail_scr[...] = st_ref[0]

    ones = _head_ones(2 * LANES)
    y = y_ref[0]
    inv = 1.0 / R_HEAD_DIM
    mu = _head_sum64(y, ones) * inv
    d = y - mu
    var = _head_sum64(d * d, ones) * inv
    yn = d * lax.rsqrt(var + GN_EPS) * lng_ref[...] + lnb_ref[...]
    yr = (yn + bonus_ref[0].astype(F32)) * g_ref[0].astype(F32)
    y_b = _bdot(yr, wbr_ref[...])
    y_a = _bdot(hm_ref[0], wbm_ref[...])
    dm = x_ref.shape[2]
    merged = _sigmoid(zg_ref[0, :, 0:dm].astype(F32)) * y_a + _sigmoid(zg_ref[0, :, dm:2 * dm].astype(F32)) * y_b
    x1 = x_ref[0] + _bdot(merged, wo_ref[...])

    h2 = _rmsnorm(x1, n2g_ref[...]).astype(BF16)
    uc = jnp.dot(h2, wup_ref[:, 0:dff], preferred_element_type=F32)
    tail = tail_scr[...]
    c = cb_ref[...] + cw_ref[F_CONV - 1:F_CONV, :] * uc
    for j in range(F_CONV - 1):
        c = c + cw_ref[j:j + 1, :] * _shift_rows(uc, tail, (F_CONV - 1 - j) * nb)
    tail_scr[...] = uc[uc.shape[0] - rt:]
    ug = jnp.dot(h2, wup_ref[:, dff:2 * dff], preferred_element_type=F32)
    x2 = x1 + _bdot(_gelu_tanh(c) * ug, fd_ref[...])
    gate = _sigmoid(_bdot(_rmsnorm(x2, png_ref[...]), pgw_ref[...]))
    x3 = x2 + gate * _bdot(pe_ref[0], ppj_ref[...])
    o_ref[0] = _rmsnorm(x3, fng_ref[...])

    @pl.when(pl.program_id(1) == pl.num_programs(1) - 1)
    def _():
        tail_ref[0] = tail_scr[...]


def _back(x, hm, y, bonus, g, zg, pe, state, wts, nb, tr, name):
    gg, r, d = x.shape
    dff = wts["f_down"].shape[0]
    pdim = pe.shape[2]
    rt = state.shape[1]
    w = R_WIDTH
    assert r % tr == 0 and tr >= rt
    tok = lambda c: pl.BlockSpec((1, tr, c), lambda a, i: (a, i, 0))
    const = lambda shp: pl.BlockSpec(shp, lambda a, i: tuple(0 for _ in shp), pipeline_mode=pl.Buffered(1))
    tailspec = pl.BlockSpec((1, rt, dff), lambda a, i: (a, 0, 0))
    return pl.pallas_call(
        functools.partial(_back_kernel, nb=nb, rt=rt, dff=dff),
        grid=(gg, r // tr),
        in_specs=[tok(d), tok(M_WIDTH), tok(w), tok(w), tok(w), tok(2 * d), tok(pdim), tailspec,
                  const((1, w)), const((1, w)), const((M_WIDTH, d)), const((w, d)), const((d, d)),
                  const((1, d)), const((d, 2 * dff)), const((SUBLANES, dff)), const((1, dff)), const((dff, d)),
                  const((1, d)), const((d, d)), const((pdim, d)), const((1, d))],
        out_specs=[tok(d), tailspec],
        out_shape=[jax.ShapeDtypeStruct((gg, r, d), F32), jax.ShapeDtypeStruct((gg, rt, dff), F32)],
        scratch_shapes=[pltpu.VMEM((rt, dff), F32)],
        compiler_params=_cparams("parallel", "arbitrary"),
        name=name,
    )(x, hm, y, bonus, g, zg, pe, state, wts["r_ln_g"], wts["r_ln_b"], wts["w_branch_m"], wts["w_branch_r"],
      wts["w_out"], wts["norm2_g"], wts["f_up"], wts["f_conv_w"], wts["f_conv_b"], wts["f_down"],
      wts["ple_norm_g"], wts["ple_gate_w"], wts["ple_proj"], wts["final_norm_g"])


def _pad_rows(w, rows):
    return jnp.pad(w, ((0, rows - w.shape[0]), (0, 0)))


def _layer(x, pe, st, wts, nb):
    g, r, d = x.shape
    n = g * r
    nseq = g * nb
    nsteps = r // nb
    xf = x.reshape(n, d)
    flat = lambda a: a.reshape(n, -1)

    tr = FRONT_ROWS
    (qk, zv, zo, zg, zif, rr, rw, rk, rv, ra, rb, rg, bonus, mtail, rtail) = _front(
        x, st["mconv"], st["rshift"], wts, nb, tr, "front")

    to_seq = (lambda a: a.reshape(nseq, nsteps, a.shape[-1])) if nb == 1 else \
        (lambda a: a.reshape(nsteps, nseq, a.shape[-1]).transpose(1, 0, 2))
    from_seq = (lambda a: a.reshape(n, -1)) if nb == 1 else (lambda a: a.transpose(1, 0, 2).reshape(n, -1))

    hm, mc, mn, mm = _mlstm(to_seq(flat(qk)), to_seq(flat(zv)), to_seq(flat(zo)), to_seq(flat(zif)),
                            wts["m_gate_bias"], wts["m_norm_g"], st["mC"], st["mn"], st["mm"], "mlstm")

    y, s_new = _rscan([to_seq(flat(o)) for o in (rr, rw, rk, rv, ra, rb)], st["rS"], "rscan")

    grp = lambda a: a.reshape(g, r, a.shape[-1])
    tr_b = max(FFN_ROWS // (1 if nb == 1 else 2), st["fconv"].shape[1])
    yout, utail = _back(x, grp(from_seq(hm)), grp(from_seq(y)), bonus, rg, zg, pe, st["fconv"], wts, nb, tr_b, "back")
    return yout, dict(mtail=mtail, rtail=rtail, utail=utail, mC=mc, mn=mn, mm=mm, rS=s_new)


def kernel(x_prompt, x_sample, state_mlstm_conv, state_mlstm_C, state_mlstm_n, state_mlstm_m,
           state_rwkv_shift, state_rwkv_S, state_ffn_conv, p_prompt, p_sample,
           norm1_g, w_in, m_conv_w, m_conv_b, m_i_bias, m_f_bias, m_norm_g, w_branch_m,
           r_mix, r_w0, r_w2, r_a0, r_a2, r_g2, r_kk, r_ka, r_rk, r_ln_g, r_ln_b, w_branch_r,
           w_out, norm2_g, f_up, f_conv_w, f_conv_b, f_down, ple_norm_g, ple_gate_w, ple_proj,
           final_norm_g):
    assert norm1_g.shape[0] == 1, "single trunk layer"
    bp, tp, d = x_prompt.shape
    bs, ts, _ = x_sample.shape
    dff = f_down.shape[1]
    mw2 = 2 * M_WIDTH
    row = lambda a: a.reshape(1, -1).astype(F32)

    wi = w_in[0]
    c_if = 4 * M_WIDTH
    c_r = c_if + 2 * M_HEADS
    c_g = c_r + R_COLS
    lora = jnp.zeros((LANES, R_WIDTH), F32)
    wts = dict(
        norm1_g=row(norm1_g[0]),
        w_m=wi[:, :c_if].astype(BF16), w_r=wi[:, c_r:c_g].astype(BF16), w_g=wi[:, c_g:].astype(BF16),
        w_if=jnp.pad(wi[:, c_if:c_r], ((0, 0), (0, LANES - 2 * M_HEADS))).astype(BF16),
        m_conv_w=_pad_rows(m_conv_w[0], SUBLANES), m_conv_b=row(m_conv_b[0]),
        m_qk_scale=jnp.concatenate([jnp.ones((1, M_WIDTH), F32),
                                    jnp.full((1, M_WIDTH), M_HEAD_DIM ** -0.5, F32)], axis=1),
        m_gate_bias=jnp.pad(jnp.concatenate([m_i_bias[0], m_f_bias[0]]).reshape(1, -1),
                            ((0, 0), (0, LANES - 2 * M_HEADS))),
        m_norm_g=jnp.broadcast_to(m_norm_g[0].reshape(M_HEADS, M_HEAD_DIM, 1), (M_HEADS, M_HEAD_DIM, MLSTM_CHUNK)),
        w_branch_m=w_branch_m[0].astype(BF16),
        r_mix=row(r_mix[0]), r_w0=row(r_w0[0]), r_a0=row(r_a0[0]),
        r_w2=lora.at[:R_DECAY_LORA].set(r_w2[0]).astype(BF16),
        r_a2=lora.at[R_DECAY_LORA:R_DECAY_LORA + R_A_LORA].set(r_a2[0]).astype(BF16),
        r_g2=r_g2[0].astype(BF16), r_kk=row(r_kk[0]), r_ka=row(r_ka[0]), r_rk=row(r_rk[0]),
        r_ln_g=row(r_ln_g[0]), r_ln_b=row(r_ln_b[0]), w_branch_r=w_branch_r[0].astype(BF16),
        w_out=w_out[0].astype(BF16), norm2_g=row(norm2_g[0]), f_up=f_up[0].astype(BF16),
        f_conv_w=_pad_rows(f_conv_w[0], SUBLANES), f_conv_b=row(f_conv_b[0]),
        f_down=f_down[0].astype(BF16), ple_norm_g=row(ple_norm_g[0]), ple_gate_w=ple_gate_w[0].astype(BF16),
        ple_proj=ple_proj[0].astype(BF16), final_norm_g=row(final_norm_g),
    )

    def seq_states(c, nv, m):
        b = c.shape[0]
        nv8 = jnp.pad(nv, ((0, 0), (0, SUBLANES - M_HEADS), (0, 0)))
        m8 = jnp.pad(jnp.broadcast_to(m[:, :, None], (b, M_HEADS, LANES)), ((0, 0), (0, SUBLANES - M_HEADS), (0, 0)))
        return c, nv8, m8

    zc, zn, zm = seq_states(jnp.zeros((bp, M_HEADS, M_HEAD_DIM, M_HEAD_DIM), F32),
                            jnp.zeros((bp, M_HEADS, M_HEAD_DIM), F32), jnp.zeros((bp, M_HEADS), F32))
    st_p = dict(mconv=jnp.zeros((bp, SUBLANES, mw2), F32), rshift=jnp.zeros((bp, SUBLANES, R_COLS), F32),
                fconv=jnp.zeros((bp, SUBLANES, dff), F32), mC=zc, mn=zn, mm=zm,
                rS=jnp.zeros((bp, R_HEADS, R_HEAD_DIM, R_HEAD_DIM), F32))
    y_p, new_p = _layer(x_prompt, p_prompt[0], st_p, wts, 1)

    tmaj = lambda a: a.transpose(1, 0, 2).reshape(1, a.shape[0] * a.shape[1], a.shape[2])
    sc, sn, sm = seq_states(state_mlstm_C[0], state_mlstm_n[0], state_mlstm_m[0])
    st_s = dict(mconv=tmaj(state_mlstm_conv[0]), rshift=state_rwkv_shift[0][None], fconv=tmaj(state_ffn_conv[0]),
                mC=sc, mn=sn, mm=sm, rS=state_rwkv_S[0])
    y_s, new_s = _layer(tmaj(x_sample), tmaj(p_sample[0]), st_s, wts, bs)
    y_sample = y_s.reshape(ts, bs, d).transpose(1, 0, 2)

    def finish(new, b, tmajor):
        def last(tail, k):
            if tmajor:
                return tail.reshape(-1, b, tail.shape[-1])[-k:].transpose(1, 0, 2)
            return tail[:, tail.shape[1] - k:]
        return (last(new["mtail"], M_CONV - 1)[None], new["mC"][None], new["mn"][:, :M_HEADS][None],
                new["mm"][:, :M_HEADS, 0][None], last(new["rtail"], 1)[:, 0][None],
                new["rS"][None], last(new["utail"], F_CONV - 1)[None])

    return (y_p, y_sample) + finish(new_p, bp, False) + finish(new_s, bs, True)
```

```python
import functools

import jax
import jax.numpy as jnp
from jax import lax
from jax.experimental import pallas as pl
from jax.experimental.pallas import tpu as pltpu

F32 = jnp.float32
BF16 = jnp.bfloat16

M_HEADS = 4
M_HEAD_DIM = 128
M_WIDTH = M_HEADS * M_HEAD_DIM
M_CONV = 4
R_HEADS = 8
R_HEAD_DIM = 64
R_WIDTH = R_HEADS * R_HEAD_DIM
R_DECAY_LORA = 64
R_A_LORA = 64
R_GATE_LORA = 128
R_COLS = 3 * R_WIDTH + R_DECAY_LORA + R_A_LORA + R_GATE_LORA
F_CONV = 3
EPS = 1e-6
GN_EPS = 64e-5

LANES = 128
SUBLANES = 8
VMEM_LIMIT_BYTES = 56 * 1024 * 1024

FRONT_ROWS = 512
FFN_ROWS = 512
MLSTM_CHUNK = 256
MLSTM_MIN_CHUNK = 128
MLSTM_SUBCHUNKS = 8
MLSTM_SHORT_SEQS = 8
SCAN_CHUNK = 64
SCAN_BATCH_GROUP = 8
SCAN_SA_PIECES = 1
SCAN_UNROLL = 4


def _cparams(*sem):
    return pltpu.CompilerParams(dimension_semantics=sem, vmem_limit_bytes=VMEM_LIMIT_BYTES)


def _sigmoid(x):
    return 1.0 / (1.0 + jnp.exp(-x))


def _softplus(x):
    return jnp.maximum(x, 0.0) + jnp.log1p(jnp.exp(-jnp.abs(x)))


def _log_sigmoid(x):
    return -_softplus(-x)


def _gelu_tanh(x):
    return 0.5 * x * (1.0 + jnp.tanh(0.7978845608028654 * (x + 0.044715 * (x * x * x))))


def _bdot(a, b):
    return jnp.dot(a.astype(BF16), b.astype(BF16), preferred_element_type=F32)


def _rmsnorm(x, g):
    return x * lax.rsqrt(jnp.mean(x * x, axis=-1, keepdims=True) + EPS) * g


def _split_dot(x, m01, parts):
    acc = None
    r = x
    for p in range(parts):
        h = r.astype(BF16)
        d = jnp.dot(h, m01, preferred_element_type=F32)
        acc = d if acc is None else acc + d
        if p + 1 < parts:
            r = r - h.astype(F32)
    return acc


def _split_dot_lhs01(m01, x, parts):
    acc = None
    r = x
    for p in range(parts):
        h = r.astype(BF16)
        d = jnp.dot(m01, h, preferred_element_type=F32)
        acc = d if acc is None else acc + d
        if p + 1 < parts:
            r = r - h.astype(F32)
    return acc


_NN = (((1,), (0,)), ((), ()))
_NT = (((1,), (1,)), ((), ()))


def _dot_pieces(a, b, dims):
    a_hi, b_hi = a.astype(BF16), b.astype(BF16)
    a_lo, b_lo = (a - a_hi.astype(F32)).astype(BF16), (b - b_hi.astype(F32)).astype(BF16)
    dot = lambda x, y: lax.dot_general(x, y, dims, preferred_element_type=F32)
    return dot(a_hi, b_hi) + dot(a_hi, b_lo) + dot(a_lo, b_hi)


def _head_ones(n):
    r = lax.broadcasted_iota(jnp.int32, (n, n), 0) // R_HEAD_DIM
    c = lax.broadcasted_iota(jnp.int32, (n, n), 1) // R_HEAD_DIM
    return jnp.where(r == c, 1.0, 0.0).astype(BF16)


def _head_sum64(x, ones):
    wd = ones.shape[0]
    outs = [_split_dot(x[:, s:s + wd], ones, 2) for s in range(0, x.shape[1], wd)]
    return outs[0] if len(outs) == 1 else jnp.concatenate(outs, axis=1)


def _shift_rows(x, tail, s):
    n, rt = x.shape[0], tail.shape[0]
    if s % SUBLANES == 0:
        if s == n:
            return tail[rt - s:]
        return jnp.concatenate([tail[rt - s:], x[:n - s]], axis=0)
    assert s < SUBLANES
    r = pltpu.roll(x, s, 0)
    t8 = pltpu.roll(tail[rt - SUBLANES:], s, 0)
    row = lax.broadcasted_iota(jnp.int32, (SUBLANES, x.shape[1]), 0)
    first = jnp.where(row < s, t8, r[:SUBLANES])
    if n == SUBLANES:
        return first
    return jnp.concatenate([first, r[SUBLANES:]], axis=0)


def _front_kernel(x_ref, mst_ref, rst_ref, n1g_ref, wm_ref, wr_ref, wg_ref, wif_ref, cw_ref, cb_ref, qsc_ref, mix_ref, w2_ref, a2_ref,
                  g2_ref, w0_ref, a0_ref, kk_ref, ka_ref, rk_ref,
                  qk_o, zv_o, zo_o, zg_o, zif_o, r_o, w_o, k_o, v_o, a_o, b_o, g_o, bonus_o, mtail_o, rtail_o,
                  mtail_scr, rtail_scr, *, nb):
    @pl.when(pl.program_id(1) == 0)
    def _():
        mtail_scr[...] = mst_ref[0]
        rtail_scr[...] = rst_ref[0]

    h = _rmsnorm(x_ref[0], n1g_ref[...]).astype(BF16)
    proj = lambda w: jnp.dot(h, w, preferred_element_type=F32)
    tr = h.shape[0]
    mw2 = 2 * M_WIDTH

    z = proj(wr_ref[...])
    rtail = rtail_scr[...]
    prev = _shift_rows(z, rtail, nb)
    rtail_scr[...] = z[tr - rtail.shape[0]:]
    xm = z + (prev - z) * mix_ref[...]
    w = R_WIDTH
    r, kr, vr = xm[:, 0:w], xm[:, w:2 * w], xm[:, 2 * w:3 * w]
    lora_in = xm[:, 3 * w:3 * w + LANES]
    gd = xm[:, 3 * w + LANES:3 * w + 2 * LANES]
    lw = _bdot(jnp.tanh(lora_in), w2_ref[...])
    la = _bdot(lora_in, a2_ref[...])
    wlog = -_softplus(-(w0_ref[...] + lw)) - 0.5
    a = _sigmoid(a0_ref[...] + la)
    ones = _head_ones(2 * LANES)
    kk = kr * kk_ref[...]
    kk = kk / jnp.maximum(jnp.sqrt(_head_sum64(kk * kk, ones)), 1e-12)
    k2 = kr * (1.0 + (a - 1.0) * ka_ref[...])
    r_o[0] = r
    w_o[0] = jnp.exp(-jnp.exp(wlog))
    k_o[0] = k2
    v_o[0] = vr
    a_o[0] = -kk
    b_o[0] = kk * a
    g_o[0] = _bdot(_sigmoid(gd), g2_ref[...]).astype(g_o.dtype)
    bonus_o[0] = (_head_sum64(r * k2 * rk_ref[...], ones) * vr).astype(bonus_o.dtype)

    zqk = proj(wm_ref[:, 0:mw2])
    mtail = mtail_scr[...]
    y = cb_ref[...] + cw_ref[M_CONV - 1:M_CONV, :] * zqk
    for j in range(M_CONV - 1):
        y = y + cw_ref[j:j + 1, :] * _shift_rows(zqk, mtail, (M_CONV - 1 - j) * nb)
    mtail_scr[...] = zqk[tr - mtail.shape[0]:]
    qk_o[0] = y * _sigmoid(y) * qsc_ref[...]
    zv_o[0] = proj(wm_ref[:, mw2:mw2 + M_WIDTH])
    zo_o[0] = proj(wm_ref[:, mw2 + M_WIDTH:mw2 + 2 * M_WIDTH])
    zg_o[0] = proj(wg_ref[...]).astype(zg_o.dtype)
    zif_o[0] = proj(wif_ref[...])

    @pl.when(pl.program_id(1) == pl.num_programs(1) - 1)
    def _():
        mtail_o[0] = mtail_scr[...]
        rtail_o[0] = rtail_scr[...]


def _front(x, mstate, rstate, wts, nb, tr, name):
    g, r, d = x.shape
    rt_m, rt_r = mstate.shape[1], rstate.shape[1]
    assert r % tr == 0 and tr >= max(rt_m, rt_r)
    mw2, w = 2 * M_WIDTH, R_WIDTH
    tok = lambda c: pl.BlockSpec((1, tr, c), lambda a, i: (a, i, 0))
    const = lambda shp: pl.BlockSpec(shp, lambda a, i: tuple(0 for _ in shp), pipeline_mode=pl.Buffered(1))
    tail = lambda rows, c: pl.BlockSpec((1, rows, c), lambda a, i: (a, 0, 0))
    out_cols = [(mw2, F32), (M_WIDTH, F32), (M_WIDTH, F32), (2 * d, BF16), (LANES, F32)] \
        + [(w, F32)] * 6 + [(w, BF16), (w, BF16)]
    return pl.pallas_call(
        functools.partial(_front_kernel, nb=nb),
        grid=(g, r // tr),
        in_specs=[tok(d), tail(rt_m, mw2), tail(rt_r, R_COLS), const((1, d)),
                  const((d, 2 * mw2)), const((d, R_COLS)), const((d, 2 * d)), const((d, LANES)),
                  const((SUBLANES, mw2)), const((1, mw2)), const((1, mw2)), const((1, R_COLS)),
                  const((LANES, w)), const((LANES, w)), const((LANES, w)),
                  const((1, w)), const((1, w)), const((1, w)), const((1, w)), const((1, w))],
        out_specs=[tok(c) for c, _ in out_cols] + [tail(rt_m, mw2), tail(rt_r, R_COLS)],
        out_shape=[jax.ShapeDtypeStruct((g, r, c), dt) for c, dt in out_cols]
        + [jax.ShapeDtypeStruct((g, rt_m, mw2), F32), jax.ShapeDtypeStruct((g, rt_r, R_COLS), F32)],
        scratch_shapes=[pltpu.VMEM((rt_m, mw2), F32), pltpu.VMEM((rt_r, R_COLS), F32)],
        compiler_params=_cparams("parallel", "arbitrary"),
        name=name,
    )(x, mstate, rstate, wts["norm1_g"], wts["w_m"], wts["w_r"], wts["w_g"], wts["w_if"], wts["m_conv_w"], wts["m_conv_b"], wts["m_qk_scale"],
      wts["r_mix"], wts["r_w2"], wts["r_a2"], wts["r_g2"], wts["r_w0"], wts["r_a0"], wts["r_kk"], wts["r_ka"],
      wts["r_rk"])


def _mlstm_chunks(chunks, ng_ref, lb, lp):
    hd, nh = M_HEAD_DIM, M_HEADS
    row = lax.broadcasted_iota(jnp.int32, (lp, LANES), 0)
    lane = lax.broadcasted_iota(jnp.int32, (lp, LANES), 1)
    valid = row < lb
    is_forget = valid & (lane >= nh) & (lane < 2 * nh)
    sq_r = lax.broadcasted_iota(jnp.int32, (lp, lp), 0)
    sq_c = lax.broadcasted_iota(jnp.int32, (lp, lp), 1)
    tril = jnp.where(sq_r >= sq_c, 1.0, 0.0).astype(BF16)
    keep = sq_r <= sq_c

    xgs = []
    for qk, v, gates, state in chunks:
        li = jnp.where(valid, gates, -jnp.inf)
        lf = jnp.where(is_forget, _log_sigmoid(gates), 0.0)
        bcum = _split_dot_lhs01(tril, lf, 2)
        xg = jnp.where(lane < nh, li, bcum)
        xgs.append((xg, xg.T))

    items = []
    for (qk, v, gates, (ct, n, m)), (xg, xg_t) in zip(chunks, xgs):
        for h in range(nh):
            brow = xg_t[nh + h:nh + h + 1, :]
            irow = xg_t[h:h + 1, :]
            ccol = xg[:, h:h + 1] - xg[:, nh + h:nh + h + 1]
            q = qk[:, h * hd:(h + 1) * hd]
            k = qk[:, M_WIDTH + h * hd:M_WIDTH + (h + 1) * hd]
            qb, kb = q.astype(BF16), k.astype(BF16)
            v_t = v[:, h * hd:(h + 1) * hd].T
            dm = jnp.where(keep, brow + ccol, -jnp.inf)
            inter = brow + m[h]
            mt = jnp.maximum(inter, jnp.max(dm, axis=0, keepdims=True))
            s_t = lax.dot_general(kb, qb, _NT, preferred_element_type=F32)
            cq = lax.dot_general(ct[h].astype(BF16), qb, _NT, preferred_element_type=F32)
            qn = _dot_pieces(jnp.broadcast_to(n[h], (SUBLANES, hd)), q, _NT)[0:1, :]
            btot = brow[:, lp - 1:lp]
            wlog = btot - brow + irow
            m_h = jnp.maximum(btot + m[h], jnp.max(wlog, axis=1, keepdims=True))
            items.append(dict(h=h, k=k, kb=kb, v_t=v_t, dm=dm, inter=inter, mt=mt, s_t=s_t, cq=cq, qn=qn,
                              btot=btot, wlog=wlog, m_h=m_h, ct=ct[h], n=n[h], m=m[h]))

    for it in items:
        p_t = jnp.exp(it["dm"] - it["mt"]) * it["s_t"]
        sc = jnp.exp(it["inter"] - it["mt"])
        num = sc * it["cq"] + jnp.dot(it["v_t"].astype(BF16), p_t.astype(BF16), preferred_element_type=F32)
        den = sc * it["qn"] + jnp.sum(p_t, axis=0, keepdims=True)
        hh = num / jnp.maximum(jnp.abs(den), jnp.exp(-it["mt"]))
        mu = jnp.mean(hh, axis=0, keepdims=True)
        d = hh - mu
        it["out"] = d * lax.rsqrt(jnp.mean(d * d, axis=0, keepdims=True) + EPS) * ng_ref[it["h"], :, 0:lp]

    for it in items:
        s0 = jnp.exp(it["btot"] + it["m"] - it["m_h"])
        ws = jnp.exp(it["wlog"] - it["m_h"])
        it["ct_new"] = s0 * it["ct"] + jnp.dot((it["v_t"] * ws).astype(BF16), it["kb"], preferred_element_type=F32)
        it["n_new"] = s0 * it["n"] + _dot_pieces(jnp.broadcast_to(ws, (SUBLANES, lp)), it["k"], _NN)[0:1, :]

    res = []
    for c in range(len(chunks)):
        its = items[c * nh:(c + 1) * nh]
        res.append(([it["out"] for it in its],
                    ([it["ct_new"] for it in its], [it["n_new"] for it in its], [it["m_h"] for it in its])))
    return res


def _mlstm_kernel(qk_ref, v_ref, og_ref, gt_ref, gb_ref, ng_ref, c0_ref, n0_ref, m0_ref,
                  h_ref, ct_ref, nt_ref, mt_ref, c_scr, n_scr, m_scr, *, lb, nsub, bb):
    lp, hd, nh = max(lb, MLSTM_MIN_CHUNK), M_HEAD_DIM, M_HEADS

    @pl.when(pl.program_id(1) == 0)
    def _():
        for si in range(bb):
            for h in range(nh):
                c_scr[si, h] = c0_ref[si, h].T
        n_scr[...] = n0_ref[...]
        m_scr[...] = m0_ref[...]

    def pad(x):
        if lb == lp:
            return x
        return jnp.concatenate([x, jnp.zeros((lp - lb, x.shape[1]), x.dtype)], axis=0)

    states = []
    for si in range(bb):
        n_all, m_all = n_scr[si], m_scr[si]
        states.append(([c_scr[si, h] for h in range(nh)], [n_all[h:h + 1, :] for h in range(nh)],
                       [m_all[h:h + 1, 0:1] for h in range(nh)]))
    for ci in range(nsub):
        rows = slice(ci * lb, (ci + 1) * lb)
        res = _mlstm_chunks([(pad(qk_ref[si, rows, :]), pad(v_ref[si, rows, :]),
                              pad(gt_ref[si, rows, :]) + gb_ref[...], states[si]) for si in range(bb)],
                            ng_ref, lb, lp)
        for si, (outs, state) in enumerate(res):
            states[si] = state
            for h in range(nh):
                cols = slice(h * hd, (h + 1) * hd)
                h_ref[si, rows, cols] = _sigmoid(og_ref[si, rows, cols]) * outs[h].T[:lb]
    for si, (ct, n, m) in enumerate(states):
        for h in range(nh):
            c_scr[si, h] = ct[h]
        n_scr[si, 0:nh, :] = jnp.concatenate(n, axis=0)
        m_scr[si, 0:nh, :] = jnp.concatenate([jnp.broadcast_to(x, (1, LANES)) for x in m], axis=0)

    @pl.when(pl.program_id(1) == pl.num_programs(1) - 1)
    def _():
        for si in range(bb):
            for h in range(nh):
                ct_ref[si, h] = c_scr[si, h].T
        nt_ref[...] = n_scr[...]
        mt_ref[...] = m_scr[...]


def _mlstm(qk, v, og, gt, gbias, ng, c0, n0, m0, name):
    b, t, _ = qk.shape
    lb = min(t, MLSTM_CHUNK)
    nsub = min(t // lb, MLSTM_SUBCHUNKS)
    bb = 1 if nsub > 1 else min(b, MLSTM_SHORT_SEQS)
    assert t % (lb * nsub) == 0 and b % bb == 0
    tok = lambda w: pl.BlockSpec((bb, lb * nsub, w), lambda a, c: (a, c, 0))
    const = lambda shp: pl.BlockSpec(shp, lambda a, c: tuple(0 for _ in shp))
    st4 = pl.BlockSpec((bb, M_HEADS, M_HEAD_DIM, M_HEAD_DIM), lambda a, c: (a, 0, 0, 0))
    st3 = pl.BlockSpec((bb, SUBLANES, LANES), lambda a, c: (a, 0, 0))
    return pl.pallas_call(
        functools.partial(_mlstm_kernel, lb=lb, nsub=nsub, bb=bb),
        grid=(b // bb, t // (lb * nsub)),
        in_specs=[tok(2 * M_WIDTH), tok(M_WIDTH), tok(M_WIDTH), tok(LANES),
                  const((1, LANES)), const((M_HEADS, M_HEAD_DIM, MLSTM_CHUNK)), st4, st3, st3],
        out_specs=[tok(M_WIDTH), st4, st3, st3],
        out_shape=[jax.ShapeDtypeStruct((b, t, M_WIDTH), F32),
                   jax.ShapeDtypeStruct((b, M_HEADS, M_HEAD_DIM, M_HEAD_DIM), F32),
                   jax.ShapeDtypeStruct((b, SUBLANES, LANES), F32),
                   jax.ShapeDtypeStruct((b, SUBLANES, LANES), F32)],
        scratch_shapes=[pltpu.VMEM((bb, M_HEADS, M_HEAD_DIM, M_HEAD_DIM), F32),
                        pltpu.VMEM((bb, SUBLANES, LANES), F32),
                        pltpu.VMEM((bb, SUBLANES, LANES), F32)],
        compiler_params=_cparams("parallel", "arbitrary"),
        name=name,
    )(qk, v, og, gt, gbias, ng, c0, n0, m0)


def _rscan_kernel(r_ref, w_ref, k_ref, v_ref, a_ref, b_ref, s0_ref, y_ref, st_ref, s_scr, y_scr, *, bg, steps):
    hp = R_HEADS // 2
    npair = bg * hp
    nquad = npair // 2
    hd = R_HEAD_DIM
    tc = pl.program_id(1)
    pairs = [(bi, p) for bi in range(bg) for p in range(hp)]
    sl = lambda q: slice(q * hd, (q + 1) * hd)

    @pl.when(tc == 0)
    def _():
        for q, (bi, p) in enumerate(pairs):
            s_scr[sl(q), :] = jnp.concatenate([s0_ref[bi, 2 * p], s0_ref[bi, 2 * p + 1]], axis=1)

    y_scr[...] = jnp.zeros(y_scr.shape, F32)
    ones = _head_ones(2 * LANES)
    lane = lax.broadcasted_iota(jnp.int32, (hd, LANES), 1) & (hd - 1)
    diag = lane == lax.broadcasted_iota(jnp.int32, (hd, LANES), 0)

    def side_by_side(xs):
        return jnp.concatenate([jnp.concatenate([xs[2 * u], xs[2 * u + 1]], axis=1) for u in range(nquad)], axis=0)

    def pair_of(res, q):
        return res[(q // 2) * hd:(q // 2 + 1) * hd, (q % 2) * LANES:(q % 2 + 1) * LANES]

    def step8(t8, carry):
        tiles = [[ref[bi, t8, :, p * LANES:(p + 1) * LANES] for bi, p in pairs]
                 for ref in (r_ref, w_ref, k_ref, v_ref, a_ref, b_ref)]
        s_cur = [s_scr[sl(q), :] for q in range(npair)]
        for j in range(SUBLANES):
            r_t, w_t, k_t, v_t, a_t, b_t = [[tl[q][j:j + 1, :] for q in range(npair)] for tl in tiles]
            prod = side_by_side([s_cur[q] * a_t[q] for q in range(npair)])
            sa = _split_dot(prod, ones, SCAN_SA_PIECES)
            vdiag = side_by_side([jnp.where(diag, v_t[q], 0.0) for q in range(npair)])
            vcol = jnp.dot(vdiag.astype(BF16), ones, preferred_element_type=F32)
            s_cur = [s_cur[q] * w_t[q] + pair_of(sa, q) * b_t[q] + pair_of(vcol, q) * k_t[q] for q in range(npair)]
            outs = side_by_side([s_cur[q] * r_t[q] for q in range(npair)])
            yb = jnp.dot(outs.astype(BF16), ones, preferred_element_type=F32)
            hit = lane == t8 * SUBLANES + j
            for q in range(npair):
                pltpu.store(y_scr.at[sl(q), :], pair_of(yb, q), mask=hit)
        for q in range(npair):
            s_scr[sl(q), :] = s_cur[q]
        return carry

    trips = steps // SUBLANES
    lax.fori_loop(0, trips, step8, 0, unroll=SCAN_UNROLL if trips % SCAN_UNROLL == 0 else 1)

    low = lax.broadcasted_iota(jnp.int32, (hd, LANES), 1) < hd
    for u in range(nquad):
        yt = y_scr[u * LANES:(u + 1) * LANES, :].T
        top, bot = yt[:hd], yt[hd:]
        bi, p0 = pairs[2 * u]
        y_ref[bi, :, p0 * LANES:(p0 + 1) * LANES] = jnp.where(low, top, pltpu.roll(bot, hd, 1))[:steps]
        y_ref[bi, :, (p0 + 1) * LANES:(p0 + 2) * LANES] = jnp.where(low, pltpu.roll(top, hd, 1), bot)[:steps]

    @pl.when(tc == pl.num_programs(1) - 1)
    def _():
        for q, (bi, p) in enumerate(pairs):
            s_fin = s_scr[sl(q), :]
            st_ref[bi, 2 * p] = s_fin[:, :hd]
            st_ref[bi, 2 * p + 1] = s_fin[:, hd:]


def _rscan(ops, s0, name):
    nbatch, nsteps, _ = ops[0].shape
    bg = SCAN_BATCH_GROUP
    steps = min(nsteps, SCAN_CHUNK)
    assert nbatch % bg == 0 and nsteps % steps == 0 and steps % SUBLANES == 0
    rows = bg * (R_HEADS // 2) * R_HEAD_DIM
    ops = [o.reshape(nbatch, nsteps // SUBLANES, SUBLANES, R_WIDTH) for o in ops]
    opspec = pl.BlockSpec((bg, steps // SUBLANES, SUBLANES, R_WIDTH), lambda g, c: (g, c, 0, 0))
    stspec = pl.BlockSpec((bg, R_HEADS, R_HEAD_DIM, R_HEAD_DIM), lambda g, c: (g, 0, 0, 0))
    return pl.pallas_call(
        functools.partial(_rscan_kernel, bg=bg, steps=steps),
        grid=(nbatch // bg, nsteps // steps),
        in_specs=[opspec] * 6 + [stspec],
        out_specs=[pl.BlockSpec((bg, steps, R_WIDTH), lambda g, c: (g, c, 0)), stspec],
        out_shape=[jax.ShapeDtypeStruct((nbatch, nsteps, R_WIDTH), F32),
                   jax.ShapeDtypeStruct(s0.shape, F32)],
        scratch_shapes=[pltpu.VMEM((rows, LANES), F32), pltpu.VMEM((rows, LANES), F32)],
        compiler_params=_cparams("parallel", "arbitrary"),
        name=name,
    )(*ops, s0)


def _back_kernel(x_ref, hm_ref, y_ref, bonus_ref, g_ref, zg_ref, pe_ref, st_ref, lng_ref, lnb_ref, wbm_ref, wbr_ref,
                 wo_ref, n2g_ref, wup_ref, cw_ref, cb_ref, fd_ref, png_ref, pgw_ref, ppj_ref, fng_ref,
                 o_ref, tail_ref, tail_scr, *, nb, rt, dff):
    @pl.when(pl.program_id(1) == 0)
    def _():
        tail_scr[...] = st_ref[0]

    ones = _head_ones(2 * LANES)
    y = y_ref[0]
    inv = 1.0 / R_HEAD_DIM
    mu = _head_sum64(y, ones) * inv
    d = y - mu
    var = _head_sum64(d * d, ones) * inv
    yn = d * lax.rsqrt(var + GN_EPS) * lng_ref[...] + lnb_ref[...]
    yr = (yn + bonus_ref[0].astype(F32)) * g_ref[0].astype(F32)
    y_b = _bdot(yr, wbr_ref[...])
    y_a = _bdot(hm_ref[0], wbm_ref[...])
    dm = x_ref.shape[2]
    merged = _sigmoid(zg_ref[0, :, 0:dm].astype(F32)) * y_a + _sigmoid(zg_ref[0, :, dm:2 * dm].astype(F32)) * y_b
    x1 = x_ref[0] + _bdot(merged, wo_ref[...])

    h2 = _rmsnorm(x1, n2g_ref[...]).astype(BF16)
    uc = jnp.dot(h2, wup_ref[:, 0:dff], preferred_element_type=F32)
    tail = tail_scr[...]
    c = cb_ref[...] + cw_ref[F_CONV - 1:F_CONV, :] * uc
    for j in range(F_CONV - 1):
        c = c + cw_ref[j:j + 1, :] * _shift_rows(uc, tail, (F_CONV - 1 - j) * nb)
    tail_scr[...] = uc[uc.shape[0] - rt:]
    ug = jnp.dot(h2, wup_ref[:, dff:2 * dff], preferred_element_type=F32)
    x2 = x1 + _bdot(_gelu_tanh(c) * ug, fd_ref[...])
    gate = _sigmoid(_bdot(_rmsnorm(x2, png_ref[...]), pgw_ref[...]))
    x3 = x2 + gate * _bdot(pe_ref[0], ppj_ref[...])
    o_ref[0] = _rmsnorm(x3, fng_ref[...])

    @pl.when(pl.program_id(1) == pl.num_programs(1) - 1)
    def _():
        tail_ref[0] = tail_scr[...]


def _back(x, hm, y, bonus, g, zg, pe, state, wts, nb, tr, name):
    gg, r, d = x.shape
    dff = wts["f_down"].shape[0]
    pdim = pe.shape[2]
    rt = state.shape[1]
    w = R_WIDTH
    assert r % tr == 0 and tr >= rt
    tok = lambda c: pl.BlockSpec((1, tr, c), lambda a, i: (a, i, 0))
    const = lambda shp: pl.BlockSpec(shp, lambda a, i: tuple(0 for _ in shp), pipeline_mode=pl.Buffered(1))
    tailspec = pl.BlockSpec((1, rt, dff), lambda a, i: (a, 0, 0))
    return pl.pallas_call(
        functools.partial(_back_kernel, nb=nb, rt=rt, dff=dff),
        grid=(gg, r // tr),
        in_specs=[tok(d), tok(M_WIDTH), tok(w), tok(w), tok(w), tok(2 * d), tok(pdim), tailspec,
                  const((1, w)), const((1, w)), const((M_WIDTH, d)), const((w, d)), const((d, d)),
                  const((1, d)), const((d, 2 * dff)), const((SUBLANES, dff)), const((1, dff)), const((dff, d)),
                  const((1, d)), const((d, d)), const((pdim, d)), const((1, d))],
        out_specs=[tok(d), tailspec],
        out_shape=[jax.ShapeDtypeStruct((gg, r, d), F32), jax.ShapeDtypeStruct((gg, rt, dff), F32)],
        scratch_shapes=[pltpu.VMEM((rt, dff), F32)],
        compiler_params=_cparams("parallel", "arbitrary"),
        name=name,
    )(x, hm, y, bonus, g, zg, pe, state, wts["r_ln_g"], wts["r_ln_b"], wts["w_branch_m"], wts["w_branch_r"],
      wts["w_out"], wts["norm2_g"], wts["f_up"], wts["f_conv_w"], wts["f_conv_b"], wts["f_down"],
      wts["ple_norm_g"], wts["ple_gate_w"], wts["ple_proj"], wts["final_norm_g"])


def _pad_rows(w, rows):
    return jnp.pad(w, ((0, rows - w.shape[0]), (0, 0)))


def _layer(x, pe, st, wts, nb):
    g, r, d = x.shape
    n = g * r
    nseq = g * nb
    nsteps = r // nb
    xf = x.reshape(n, d)
    flat = lambda a: a.reshape(n, -1)

    tr = FRONT_ROWS
    (qk, zv, zo, zg, zif, rr, rw, rk, rv, ra, rb, rg, bonus, mtail, rtail) = _front(
        x, st["mconv"], st["rshift"], wts, nb, tr, "front")

    to_seq = (lambda a: a.reshape(nseq, nsteps, a.shape[-1])) if nb == 1 else \
        (lambda a: a.reshape(nsteps, nseq, a.shape[-1]).transpose(1, 0, 2))
    from_seq = (lambda a: a.reshape(n, -1)) if nb == 1 else (lambda a: a.transpose(1, 0, 2).reshape(n, -1))

    hm, mc, mn, mm = _mlstm(to_seq(flat(qk)), to_seq(flat(zv)), to_seq(flat(zo)), to_seq(flat(zif)),
                            wts["m_gate_bias"], wts["m_norm_g"], st["mC"], st["mn"], st["mm"], "mlstm")

    y, s_new = _rscan([to_seq(flat(o)) for o in (rr, rw, rk, rv, ra, rb)], st["rS"], "rscan")

    grp = lambda a: a.reshape(g, r, a.shape[-1])
    tr_b = max(FFN_ROWS // (1 if nb == 1 else 2), st["fconv"].shape[1])
    yout, utail = _back(x, grp(from_seq(hm)), grp(from_seq(y)), bonus, rg, zg, pe, st["fconv"], wts, nb, tr_b, "back")
    return yout, dict(mtail=mtail, rtail=rtail, utail=utail, mC=mc, mn=mn, mm=mm, rS=s_new)


def kernel(x_prompt, x_sample, state_mlstm_conv, state_mlstm_C, state_mlstm_n, state_mlstm_m,
           state_rwkv_shift, state_rwkv_S, state_ffn_conv, p_prompt, p_sample,
           norm1_g, w_in, m_conv_w, m_conv_b, m_i_bias, m_f_bias, m_norm_g, w_branch_m,
           r_mix, r_w0, r_w2, r_a0, r_a2, r_g2, r_kk, r_ka, r_rk, r_ln_g, r_ln_b, w_branch_r,
           w_out, norm2_g, f_up, f_conv_w, f_conv_b, f_down, ple_norm_g, ple_gate_w, ple_proj,
           final_norm_g):
    assert norm1_g.shape[0] == 1, "single trunk layer"
    bp, tp, d = x_prompt.shape
    bs, ts, _ = x_sample.shape
    dff = f_down.shape[1]
    mw2 = 2 * M_WIDTH
    row = lambda a: a.reshape(1, -1).astype(F32)

    wi = w_in[0]
    c_if = 4 * M_WIDTH
    c_r = c_if + 2 * M_HEADS
    c_g = c_r + R_COLS
    lora = jnp.zeros((LANES, R_WIDTH), F32)
    wts = dict(
        norm1_g=row(norm1_g[0]),
        w_m=wi[:, :c_if].astype(BF16), w_r=wi[:, c_r:c_g].astype(BF16), w_g=wi[:, c_g:].astype(BF16),
        w_if=jnp.pad(wi[:, c_if:c_r], ((0, 0), (0, LANES - 2 * M_HEADS))).astype(BF16),
        m_conv_w=_pad_rows(m_conv_w[0], SUBLANES), m_conv_b=row(m_conv_b[0]),
        m_qk_scale=jnp.concatenate([jnp.ones((1, M_WIDTH), F32),
                                    jnp.full((1, M_WIDTH), M_HEAD_DIM ** -0.5, F32)], axis=1),
        m_gate_bias=jnp.pad(jnp.concatenate([m_i_bias[0], m_f_bias[0]]).reshape(1, -1),
                            ((0, 0), (0, LANES - 2 * M_HEADS))),
        m_norm_g=jnp.broadcast_to(m_norm_g[0].reshape(M_HEADS, M_HEAD_DIM, 1), (M_HEADS, M_HEAD_DIM, MLSTM_CHUNK)),
        w_branch_m=w_branch_m[0].astype(BF16),
        r_mix=row(r_mix[0]), r_w0=row(r_w0[0]), r_a0=row(r_a0[0]),
        r_w2=lora.at[:R_DECAY_LORA].set(r_w2[0]).astype(BF16),
        r_a2=lora.at[R_DECAY_LORA:R_DECAY_LORA + R_A_LORA].set(r_a2[0]).astype(BF16),
        r_g2=r_g2[0].astype(BF16), r_kk=row(r_kk[0]), r_ka=row(r_ka[0]), r_rk=row(r_rk[0]),
        r_ln_g=row(r_ln_g[0]), r_ln_b=row(r_ln_b[0]), w_branch_r=w_branch_r[0].astype(BF16),
        w_out=w_out[0].astype(BF16), norm2_g=row(norm2_g[0]), f_up=f_up[0].astype(BF16),
        f_conv_w=_pad_rows(f_conv_w[0], SUBLANES), f_conv_b=row(f_conv_b[0]),
        f_down=f_down[0].astype(BF16), ple_norm_g=row(ple_norm_g[0]), ple_gate_w=ple_gate_w[0].astype(BF16),
        ple_proj=ple_proj[0].astype(BF16), final_norm_g=row(final_norm_g),
    )

    def seq_states(c, nv, m):
        b = c.shape[0]
        nv8 = jnp.pad(nv, ((0, 0), (0, SUBLANES - M_HEADS), (0, 0)))
        m8 = jnp.pad(jnp.broadcast_to(m[:, :, None], (b, M_HEADS, LANES)), ((0, 0), (0, SUBLANES - M_HEADS), (0, 0)))
        return c, nv8, m8

    zc, zn, zm = seq_states(jnp.zeros((bp, M_HEADS, M_HEAD_DIM, M_HEAD_DIM), F32),
                            jnp.zeros((bp, M_HEADS, M_HEAD_DIM), F32), jnp.zeros((bp, M_HEADS), F32))
    st_p = dict(mconv=jnp.zeros((bp, SUBLANES, mw2), F32), rshift=jnp.zeros((bp, SUBLANES, R_COLS), F32),
                fconv=jnp.zeros((bp, SUBLANES, dff), F32), mC=zc, mn=zn, mm=zm,
                rS=jnp.zeros((bp, R_HEADS, R_HEAD_DIM, R_HEAD_DIM), F32))
    y_p, new_p = _layer(x_prompt, p_prompt[0], st_p, wts, 1)

    tmaj = lambda a: a.transpose(1, 0, 2).reshape(1, a.shape[0] * a.shape[1], a.shape[2])
    sc, sn, sm = seq_states(state_mlstm_C[0], state_mlstm_n[0], state_mlstm_m[0])
    st_s = dict(mconv=tmaj(state_mlstm_conv[0]), rshift=state_rwkv_shift[0][None], fconv=tmaj(state_ffn_conv[0]),
                mC=sc, mn=sn, mm=sm, rS=state_rwkv_S[0])
    y_s, new_s = _layer(tmaj(x_sample), tmaj(p_sample[0]), st_s, wts, bs)
    y_sample = y_s.reshape(ts, bs, d).transpose(1, 0, 2)

    def finish(new, b, tmajor):
        def last(tail, k):
            if tmajor:
                return tail.reshape(-1, b, tail.shape[-1])[-k:].transpose(1, 0, 2)
            return tail[:, tail.shape[1] - k:]
        return (last(new["mtail"], M_CONV - 1)[None], new["mC"][None], new["mn"][:, :M_HEADS][None],
                new["mm"][:, :M_HEADS, 0][None], last(new["rtail"], 1)[:, 0][None],
                new["rS"][None], last(new["utail"], F_CONV - 1)[None])

    return (y_p, y_sample) + finish(new_p, bp, False) + finish(new_s, bs, True)
```

```python
import functools

import jax
import jax.numpy as jnp
from jax import lax
from jax.experimental import pallas as pl
from jax.experimental.pallas import tpu as pltpu

F32 = jnp.float32
BF16 = jnp.bfloat16

M_HEADS = 4
M_HEAD_DIM = 128
M_WIDTH = M_HEADS * M_HEAD_DIM
M_CONV = 4
R_HEADS = 8
R_HEAD_DIM = 64
R_WIDTH = R_HEADS * R_HEAD_DIM
R_DECAY_LORA = 64
R_A_LORA = 64
R_GATE_LORA = 128
R_COLS = 3 * R_WIDTH + R_DECAY_LORA + R_A_LORA + R_GATE_LORA
F_CONV = 3
EPS = 1e-6
GN_EPS = 64e-5

LANES = 128
SUBLANES = 8
VMEM_LIMIT_BYTES = 56 * 1024 * 1024

FRONT_ROWS = 512
FFN_ROWS = 512
MLSTM_CHUNK = 256
MLSTM_MIN_CHUNK = 128
MLSTM_SUBCHUNKS = 4
MLSTM_LONG_SEQS = 2
MLSTM_SHORT_SEQS = 8
SCAN_CHUNK = 64
SCAN_BATCH_GROUP = 8
SCAN_SA_PIECES = 1
SCAN_UNROLL = 4


def _cparams(*sem):
    return pltpu.CompilerParams(dimension_semantics=sem, vmem_limit_bytes=VMEM_LIMIT_BYTES)


def _sigmoid(x):
    return 1.0 / (1.0 + jnp.exp(-x))


def _softplus(x):
    return jnp.maximum(x, 0.0) + jnp.log1p(jnp.exp(-jnp.abs(x)))


def _log_sigmoid(x):
    return -_softplus(-x)


def _gelu_tanh(x):
    return 0.5 * x * (1.0 + jnp.tanh(0.7978845608028654 * (x + 0.044715 * (x * x * x))))


def _bdot(a, b):
    return jnp.dot(a.astype(BF16), b.astype(BF16), preferred_element_type=F32)


def _rmsnorm(x, g):
    return x * lax.rsqrt(jnp.mean(x * x, axis=-1, keepdims=True) + EPS) * g


def _split_dot(x, m01, parts):
    acc = None
    r = x
    for p in range(parts):
        h = r.astype(BF16)
        d = jnp.dot(h, m01, preferred_element_type=F32)
        acc = d if acc is None else acc + d
        if p + 1 < parts:
            r = r - h.astype(F32)
    return acc


def _split_dot_lhs01(m01, x, parts):
    acc = None
    r = x
    for p in range(parts):
        h = r.astype(BF16)
        d = jnp.dot(m01, h, preferred_element_type=F32)
        acc = d if acc is None else acc + d
        if p + 1 < parts:
            r = r - h.astype(F32)
    return acc


_NN = (((1,), (0,)), ((), ()))
_NT = (((1,), (1,)), ((), ()))


def _dot_pieces(a, b, dims):
    a_hi, b_hi = a.astype(BF16), b.astype(BF16)
    a_lo, b_lo = (a - a_hi.astype(F32)).astype(BF16), (b - b_hi.astype(F32)).astype(BF16)
    dot = lambda x, y: lax.dot_general(x, y, dims, preferred_element_type=F32)
    return dot(a_hi, b_hi) + dot(a_hi, b_lo) + dot(a_lo, b_hi)


def _head_ones(n):
    r = lax.broadcasted_iota(jnp.int32, (n, n), 0) // R_HEAD_DIM
    c = lax.broadcasted_iota(jnp.int32, (n, n), 1) // R_HEAD_DIM
    return jnp.where(r == c, 1.0, 0.0).astype(BF16)


def _head_sum64(x, ones):
    wd = ones.shape[0]
    outs = [_split_dot(x[:, s:s + wd], ones, 2) for s in range(0, x.shape[1], wd)]
    return outs[0] if len(outs) == 1 else jnp.concatenate(outs, axis=1)


def _shift_rows(x, tail, s):
    n, rt = x.shape[0], tail.shape[0]
    if s % SUBLANES == 0:
        if s == n:
            return tail[rt - s:]
        return jnp.concatenate([tail[rt - s:], x[:n - s]], axis=0)
    assert s < SUBLANES
    r = pltpu.roll(x, s, 0)
    t8 = pltpu.roll(tail[rt - SUBLANES:], s, 0)
    row = lax.broadcasted_iota(jnp.int32, (SUBLANES, x.shape[1]), 0)
    first = jnp.where(row < s, t8, r[:SUBLANES])
    if n == SUBLANES:
        return first
    return jnp.concatenate([first, r[SUBLANES:]], axis=0)


def _front_kernel(x_ref, mst_ref, rst_ref, n1g_ref, wm_ref, wr_ref, wg_ref, wif_ref, cw_ref, cb_ref, qsc_ref, mix_ref, w2_ref, a2_ref,
                  g2_ref, w0_ref, a0_ref, kk_ref, ka_ref, rk_ref,
                  qk_o, zv_o, zo_o, zg_o, zif_o, r_o, w_o, k_o, v_o, a_o, b_o, g_o, bonus_o, mtail_o, rtail_o,
                  mtail_scr, rtail_scr, *, nb):
    @pl.when(pl.program_id(1) == 0)
    def _():
        mtail_scr[...] = mst_ref[0]
        rtail_scr[...] = rst_ref[0]

    h = _rmsnorm(x_ref[0], n1g_ref[...]).astype(BF16)
    proj = lambda w: jnp.dot(h, w, preferred_element_type=F32)
    tr = h.shape[0]
    mw2 = 2 * M_WIDTH

    z = proj(wr_ref[...])
    rtail = rtail_scr[...]
    prev = _shift_rows(z, rtail, nb)
    rtail_scr[...] = z[tr - rtail.shape[0]:]
    xm = z + (prev - z) * mix_ref[...]
    w = R_WIDTH
    r, kr, vr = xm[:, 0:w], xm[:, w:2 * w], xm[:, 2 * w:3 * w]
    lora_in = xm[:, 3 * w:3 * w + LANES]
    gd = xm[:, 3 * w + LANES:3 * w + 2 * LANES]
    lw = _bdot(jnp.tanh(lora_in), w2_ref[...])
    la = _bdot(lora_in, a2_ref[...])
    wlog = -_softplus(-(w0_ref[...] + lw)) - 0.5
    a = _sigmoid(a0_ref[...] + la)
    ones = _head_ones(2 * LANES)
    kk = kr * kk_ref[...]
    kk = kk / jnp.maximum(jnp.sqrt(_head_sum64(kk * kk, ones)), 1e-12)
    k2 = kr * (1.0 + (a - 1.0) * ka_ref[...])
    r_o[0] = r
    w_o[0] = jnp.exp(-jnp.exp(wlog))
    k_o[0] = k2
    v_o[0] = vr
    a_o[0] = -kk
    b_o[0] = kk * a
    g_o[0] = _bdot(_sigmoid(gd), g2_ref[...]).astype(g_o.dtype)
    bonus_o[0] = (_head_sum64(r * k2 * rk_ref[...], ones) * vr).astype(bonus_o.dtype)

    zqk = proj(wm_ref[:, 0:mw2])
    mtail = mtail_scr[...]
    y = cb_ref[...] + cw_ref[M_CONV - 1:M_CONV, :] * zqk
    for j in range(M_CONV - 1):
        y = y + cw_ref[j:j + 1, :] * _shift_rows(zqk, mtail, (M_CONV - 1 - j) * nb)
    mtail_scr[...] = zqk[tr - mtail.shape[0]:]
    qk_o[0] = y * _sigmoid(y) * qsc_ref[...]
    zv_o[0] = proj(wm_ref[:, mw2:mw2 + M_WIDTH])
    zo_o[0] = proj(wm_ref[:, mw2 + M_WIDTH:mw2 + 2 * M_WIDTH])
    zg_o[0] = proj(wg_ref[...]).astype(zg_o.dtype)
    zif_o[0] = proj(wif_ref[...])

    @pl.when(pl.program_id(1) == pl.num_programs(1) - 1)
    def _():
        mtail_o[0] = mtail_scr[...]
        rtail_o[0] = rtail_scr[...]


def _front(x, mstate, rstate, wts, nb, tr, name):
    g, r, d = x.shape
    rt_m, rt_r = mstate.shape[1], rstate.shape[1]
    assert r % tr == 0 and tr >= max(rt_m, rt_r)
    mw2, w = 2 * M_WIDTH, R_WIDTH
    tok = lambda c: pl.BlockSpec((1, tr, c), lambda a, i: (a, i, 0))
    const = lambda shp: pl.BlockSpec(shp, lambda a, i: tuple(0 for _ in shp), pipeline_mode=pl.Buffered(1))
    tail = lambda rows, c: pl.BlockSpec((1, rows, c), lambda a, i: (a, 0, 0))
    out_cols = [(mw2, F32), (M_WIDTH, F32), (M_WIDTH, F32), (2 * d, BF16), (LANES, F32)] \
        + [(w, F32)] * 6 + [(w, BF16), (w, BF16)]
    return pl.pallas_call(
        functools.partial(_front_kernel, nb=nb),
        grid=(g, r // tr),
        in_specs=[tok(d), tail(rt_m, mw2), tail(rt_r, R_COLS), const((1, d)),
                  const((d, 2 * mw2)), const((d, R_COLS)), const((d, 2 * d)), const((d, LANES)),
                  const((SUBLANES, mw2)), const((1, mw2)), const((1, mw2)), const((1, R_COLS)),
                  const((LANES, w)), const((LANES, w)), const((LANES, w)),
                  const((1, w)), const((1, w)), const((1, w)), const((1, w)), const((1, w))],
        out_specs=[tok(c) for c, _ in out_cols] + [tail(rt_m, mw2), tail(rt_r, R_COLS)],
        out_shape=[jax.ShapeDtypeStruct((g, r, c), dt) for c, dt in out_cols]
        + [jax.ShapeDtypeStruct((g, rt_m, mw2), F32), jax.ShapeDtypeStruct((g, rt_r, R_COLS), F32)],
        scratch_shapes=[pltpu.VMEM((rt_m, mw2), F32), pltpu.VMEM((rt_r, R_COLS), F32)],
        compiler_params=_cparams("parallel", "arbitrary"),
        name=name,
    )(x, mstate, rstate, wts["norm1_g"], wts["w_m"], wts["w_r"], wts["w_g"], wts["w_if"], wts["m_conv_w"], wts["m_conv_b"], wts["m_qk_scale"],
      wts["r_mix"], wts["r_w2"], wts["r_a2"], wts["r_g2"], wts["r_w0"], wts["r_a0"], wts["r_kk"], wts["r_ka"],
      wts["r_rk"])


def _mlstm_chunks(chunks, ng_ref, lb, lp):
    hd, nh = M_HEAD_DIM, M_HEADS
    row = lax.broadcasted_iota(jnp.int32, (lp, LANES), 0)
    lane = lax.broadcasted_iota(jnp.int32, (lp, LANES), 1)
    valid = row < lb
    is_forget = valid & (lane >= nh) & (lane < 2 * nh)
    sq_r = lax.broadcasted_iota(jnp.int32, (lp, lp), 0)
    sq_c = lax.broadcasted_iota(jnp.int32, (lp, lp), 1)
    tril = jnp.where(sq_r >= sq_c, 1.0, 0.0).astype(BF16)
    keep = sq_r <= sq_c

    xgs = []
    for qk, v, gates, state in chunks:
        li = jnp.where(valid, gates, -jnp.inf)
        lf = jnp.where(is_forget, _log_sigmoid(gates), 0.0)
        bcum = _split_dot_lhs01(tril, lf, 2)
        xg = jnp.where(lane < nh, li, bcum)
        xgs.append((xg, xg.T))

    items = []
    for (qk, v, gates, (ct, n, m)), (xg, xg_t) in zip(chunks, xgs):
        for h in range(nh):
            brow = xg_t[nh + h:nh + h + 1, :]
            irow = xg_t[h:h + 1, :]
            ccol = xg[:, h:h + 1] - xg[:, nh + h:nh + h + 1]
            q = qk[:, h * hd:(h + 1) * hd]
            k = qk[:, M_WIDTH + h * hd:M_WIDTH + (h + 1) * hd]
            qb, kb = q.astype(BF16), k.astype(BF16)
            v_t = v[:, h * hd:(h + 1) * hd].T
            dm = jnp.where(keep, brow + ccol, -jnp.inf)
            inter = brow + m[h]
            mt = jnp.maximum(inter, jnp.max(dm, axis=0, keepdims=True))
            s_t = lax.dot_general(kb, qb, _NT, preferred_element_type=F32)
            cq = lax.dot_general(ct[h].astype(BF16), qb, _NT, preferred_element_type=F32)
            qn = _dot_pieces(jnp.broadcast_to(n[h], (SUBLANES, hd)), q, _NT)[0:1, :]
            btot = brow[:, lp - 1:lp]
            wlog = btot - brow + irow
            m_h = jnp.maximum(btot + m[h], jnp.max(wlog, axis=1, keepdims=True))
            items.append(dict(h=h, k=k, kb=kb, v_t=v_t, dm=dm, inter=inter, mt=mt, s_t=s_t, cq=cq, qn=qn,
                              btot=btot, wlog=wlog, m_h=m_h, ct=ct[h], n=n[h], m=m[h]))

    for it in items:
        p_t = jnp.exp(it["dm"] - it["mt"]) * it["s_t"]
        sc = jnp.exp(it["inter"] - it["mt"])
        num = sc * it["cq"] + jnp.dot(it["v_t"].astype(BF16), p_t.astype(BF16), preferred_element_type=F32)
        den = sc * it["qn"] + jnp.sum(p_t, axis=0, keepdims=True)
        hh = num / jnp.maximum(jnp.abs(den), jnp.exp(-it["mt"]))
        mu = jnp.mean(hh, axis=0, keepdims=True)
        d = hh - mu
        it["out"] = d * lax.rsqrt(jnp.mean(d * d, axis=0, keepdims=True) + EPS) * ng_ref[it["h"], :, 0:lp]

    for it in items:
        s0 = jnp.exp(it["btot"] + it["m"] - it["m_h"])
        ws = jnp.exp(it["wlog"] - it["m_h"])
        it["ct_new"] = s0 * it["ct"] + jnp.dot((it["v_t"] * ws).astype(BF16), it["kb"], preferred_element_type=F32)
        it["n_new"] = s0 * it["n"] + _dot_pieces(jnp.broadcast_to(ws, (SUBLANES, lp)), it["k"], _NN)[0:1, :]

    res = []
    for c in range(len(chunks)):
        its = items[c * nh:(c + 1) * nh]
        res.append(([it["out"] for it in its],
                    ([it["ct_new"] for it in its], [it["n_new"] for it in its], [it["m_h"] for it in its])))
    return res


def _mlstm_kernel(qk_ref, v_ref, og_ref, gt_ref, gb_ref, ng_ref, c0_ref, n0_ref, m0_ref,
                  h_ref, ct_ref, nt_ref, mt_ref, c_scr, n_scr, m_scr, *, lb, nsub, bb):
    lp, hd, nh = max(lb, MLSTM_MIN_CHUNK), M_HEAD_DIM, M_HEADS

    @pl.when(pl.program_id(1) == 0)
    def _():
        for si in range(bb):
            for h in range(nh):
                c_scr[si, h] = c0_ref[si, h].T
        n_scr[...] = n0_ref[...]
        m_scr[...] = m0_ref[...]

    def pad(x):
        if lb == lp:
            return x
        return jnp.concatenate([x, jnp.zeros((lp - lb, x.shape[1]), x.dtype)], axis=0)

    states = []
    for si in range(bb):
        n_all, m_all = n_scr[si], m_scr[si]
        states.append(([c_scr[si, h] for h in range(nh)], [n_all[h:h + 1, :] for h in range(nh)],
                       [m_all[h:h + 1, 0:1] for h in range(nh)]))
    for ci in range(nsub):
        rows = slice(ci * lb, (ci + 1) * lb)
        res = _mlstm_chunks([(pad(qk_ref[si, rows, :]), pad(v_ref[si, rows, :]),
                              pad(gt_ref[si, rows, :]) + gb_ref[...], states[si]) for si in range(bb)],
                            ng_ref, lb, lp)
        for si, (outs, state) in enumerate(res):
            states[si] = state
            for h in range(nh):
                cols = slice(h * hd, (h + 1) * hd)
                h_ref[si, rows, cols] = _sigmoid(og_ref[si, rows, cols]) * outs[h].T[:lb]
    for si, (ct, n, m) in enumerate(states):
        for h in range(nh):
            c_scr[si, h] = ct[h]
        n_scr[si, 0:nh, :] = jnp.concatenate(n, axis=0)
        m_scr[si, 0:nh, :] = jnp.concatenate([jnp.broadcast_to(x, (1, LANES)) for x in m], axis=0)

    @pl.when(pl.program_id(1) == pl.num_programs(1) - 1)
    def _():
        for si in range(bb):
            for h in range(nh):
                ct_ref[si, h] = c_scr[si, h].T
        nt_ref[...] = n_scr[...]
        mt_ref[...] = m_scr[...]


def _mlstm(qk, v, og, gt, gbias, ng, c0, n0, m0, name):
    b, t, _ = qk.shape
    lb = min(t, MLSTM_CHUNK)
    nsub = min(t // lb, MLSTM_SUBCHUNKS)
    bb = min(b, MLSTM_LONG_SEQS if nsub > 1 else MLSTM_SHORT_SEQS)
    assert t % (lb * nsub) == 0 and b % bb == 0
    tok = lambda w: pl.BlockSpec((bb, lb * nsub, w), lambda a, c: (a, c, 0))
    const = lambda shp: pl.BlockSpec(shp, lambda a, c: tuple(0 for _ in shp))
    st4 = pl.BlockSpec((bb, M_HEADS, M_HEAD_DIM, M_HEAD_DIM), lambda a, c: (a, 0, 0, 0))
    st3 = pl.BlockSpec((bb, SUBLANES, LANES), lambda a, c: (a, 0, 0))
    return pl.pallas_call(
        functools.partial(_mlstm_kernel, lb=lb, nsub=nsub, bb=bb),
        grid=(b // bb, t // (lb * nsub)),
        in_specs=[tok(2 * M_WIDTH), tok(M_WIDTH), tok(M_WIDTH), tok(LANES),
                  const((1, LANES)), const((M_HEADS, M_HEAD_DIM, MLSTM_CHUNK)), st4, st3, st3],
        out_specs=[tok(M_WIDTH), st4, st3, st3],
        out_shape=[jax.ShapeDtypeStruct((b, t, M_WIDTH), F32),
                   jax.ShapeDtypeStruct((b, M_HEADS, M_HEAD_DIM, M_HEAD_DIM), F32),
                   jax.ShapeDtypeStruct((b, SUBLANES, LANES), F32),
                   jax.ShapeDtypeStruct((b, SUBLANES, LANES), F32)],
        scratch_shapes=[pltpu.VMEM((bb, M_HEADS, M_HEAD_DIM, M_HEAD_DIM), F32),
                        pltpu.VMEM((bb, SUBLANES, LANES), F32),
                        pltpu.VMEM((bb, SUBLANES, LANES), F32)],
        compiler_params=_cparams("parallel", "arbitrary"),
        name=name,
    )(qk, v, og, gt, gbias, ng, c0, n0, m0)


def _rscan_kernel(r_ref, w_ref, k_ref, v_ref, a_ref, b_ref, s0_ref, y_ref, st_ref, s_scr, y_scr, *, bg, steps):
    hp = R_HEADS // 2
    npair = bg * hp
    nquad = npair // 2
    hd = R_HEAD_DIM
    tc = pl.program_id(1)
    pairs = [(bi, p) for bi in range(bg) for p in range(hp)]
    sl = lambda q: slice(q * hd, (q + 1) * hd)

    @pl.when(tc == 0)
    def _():
        for q, (bi, p) in enumerate(pairs):
            s_scr[sl(q), :] = jnp.concatenate([s0_ref[bi, 2 * p], s0_ref[bi, 2 * p + 1]], axis=1)

    y_scr[...] = jnp.zeros(y_scr.shape, F32)
    ones = _head_ones(2 * LANES)
    lane = lax.broadcasted_iota(jnp.int32, (hd, LANES), 1) & (hd - 1)
    diag = lane == lax.broadcasted_iota(jnp.int32, (hd, LANES), 0)

    def side_by_side(xs):
        return jnp.concatenate([jnp.concatenate([xs[2 * u], xs[2 * u + 1]], axis=1) for u in range(nquad)], axis=0)

    def pair_of(res, q):
        return res[(q // 2) * hd:(q // 2 + 1) * hd, (q % 2) * LANES:(q % 2 + 1) * LANES]

    def step8(t8, carry):
        tiles = [[ref[bi, t8, :, p * LANES:(p + 1) * LANES] for bi, p in pairs]
                 for ref in (r_ref, w_ref, k_ref, v_ref, a_ref, b_ref)]
        s_cur = [s_scr[sl(q), :] for q in range(npair)]
        for j in range(SUBLANES):
            r_t, w_t, k_t, v_t, a_t, b_t = [[tl[q][j:j + 1, :] for q in range(npair)] for tl in tiles]
            prod = side_by_side([s_cur[q] * a_t[q] for q in range(npair)])
            sa = _split_dot(prod, ones, SCAN_SA_PIECES)
            vdiag = side_by_side([jnp.where(diag, v_t[q], 0.0) for q in range(npair)])
            vcol = jnp.dot(vdiag.astype(BF16), ones, preferred_element_type=F32)
            s_cur = [s_cur[q] * w_t[q] + pair_of(sa, q) * b_t[q] + pair_of(vcol, q) * k_t[q] for q in range(npair)]
            outs = side_by_side([s_cur[q] * r_t[q] for q in range(npair)])
            yb = jnp.dot(outs.astype(BF16), ones, preferred_element_type=F32)
            hit = lane == t8 * SUBLANES + j
            for q in range(npair):
                pltpu.store(y_scr.at[sl(q), :], pair_of(yb, q), mask=hit)
        for q in range(npair):
            s_scr[sl(q), :] = s_cur[q]
        return carry

    trips = steps // SUBLANES
    lax.fori_loop(0, trips, step8, 0, unroll=SCAN_UNROLL if trips % SCAN_UNROLL == 0 else 1)

    low = lax.broadcasted_iota(jnp.int32, (hd, LANES), 1) < hd
    for u in range(nquad):
        yt = y_scr[u * LANES:(u + 1) * LANES, :].T
        top, bot = yt[:hd], yt[hd:]
        bi, p0 = pairs[2 * u]
        y_ref[bi, :, p0 * LANES:(p0 + 1) * LANES] = jnp.where(low, top, pltpu.roll(bot, hd, 1))[:steps]
        y_ref[bi, :, (p0 + 1) * LANES:(p0 + 2) * LANES] = jnp.where(low, pltpu.roll(top, hd, 1), bot)[:steps]

    @pl.when(tc == pl.num_programs(1) - 1)
    def _():
        for q, (bi, p) in enumerate(pairs):
            s_fin = s_scr[sl(q), :]
            st_ref[bi, 2 * p] = s_fin[:, :hd]
            st_ref[bi, 2 * p + 1] = s_fin[:, hd:]


def _rscan(ops, s0, name):
    nbatch, nsteps, _ = ops[0].shape
    bg = SCAN_BATCH_GROUP
    steps = min(nsteps, SCAN_CHUNK)
    assert nbatch % bg == 0 and nsteps % steps == 0 and steps % SUBLANES == 0
    rows = bg * (R_HEADS // 2) * R_HEAD_DIM
    ops = [o.reshape(nbatch, nsteps // SUBLANES, SUBLANES, R_WIDTH) for o in ops]
    opspec = pl.BlockSpec((bg, steps // SUBLANES, SUBLANES, R_WIDTH), lambda g, c: (g, c, 0, 0))
    stspec = pl.BlockSpec((bg, R_HEADS, R_HEAD_DIM, R_HEAD_DIM), lambda g, c: (g, 0, 0, 0))
    return pl.pallas_call(
        functools.partial(_rscan_kernel, bg=bg, steps=steps),
        grid=(nbatch // bg, nsteps // steps),
        in_specs=[opspec] * 6 + [stspec],
        out_specs=[pl.BlockSpec((bg, steps, R_WIDTH), lambda g, c: (g, c, 0)), stspec],
        out_shape=[jax.ShapeDtypeStruct((nbatch, nsteps, R_WIDTH), F32),
                   jax.ShapeDtypeStruct(s0.shape, F32)],
        scratch_shapes=[pltpu.VMEM((rows, LANES), F32), pltpu.VMEM((rows, LANES), F32)],
        compiler_params=_cparams("parallel", "arbitrary"),
        name=name,
    )(*ops, s0)


def _back_kernel(x_ref, hm_ref, y_ref, bonus_ref, g_ref, zg_ref, pe_ref, st_ref, lng_ref, lnb_ref, wbm_ref, wbr_ref,
                 wo_ref, n2g_ref, wup_ref, cw_ref, cb_ref, fd_ref, png_ref, pgw_ref, ppj_ref, fng_ref,
                 o_ref, tail_ref, tail_scr, *, nb, rt, dff):
    @pl.when(pl.program_id(1) == 0)
    def _():
        tail_scr[...] = st_ref[0]

    ones = _head_ones(2 * LANES)
    y = y_ref[0]
    inv = 1.0 / R_HEAD_DIM
    mu = _head_sum64(y, ones) * inv
    d = y - mu
    var = _head_sum64(d * d, ones) * inv
    yn = d * lax.rsqrt(var + GN_EPS) * lng_ref[...] + lnb_ref[...]
    yr = (yn + bonus_ref[0].astype(F32)) * g_ref[0].astype(F32)
    y_b = _bdot(yr, wbr_ref[...])
    y_a = _bdot(hm_ref[0], wbm_ref[...])
    dm = x_ref.shape[2]
    merged = _sigmoid(zg_ref[0, :, 0:dm].astype(F32)) * y_a + _sigmoid(zg_ref[0, :, dm:2 * dm].astype(F32)) * y_b
    x1 = x_ref[0] + _bdot(merged, wo_ref[...])

    h2 = _rmsnorm(x1, n2g_ref[...]).astype(BF16)
    uc = jnp.dot(h2, wup_ref[:, 0:dff], preferred_element_type=F32)
    tail = tail_scr[...]
    c = cb_ref[...] + cw_ref[F_CONV - 1:F_CONV, :] * uc
    for j in range(F_CONV - 1):
        c = c + cw_ref[j:j + 1, :] * _shift_rows(uc, tail, (F_CONV - 1 - j) * nb)
    tail_scr[...] = uc[uc.shape[0] - rt:]
    ug = jnp.dot(h2, wup_ref[:, dff:2 * dff], preferred_element_type=F32)
    x2 = x1 + _bdot(_gelu_tanh(c) * ug, fd_ref[...])
    gate = _sigmoid(_bdot(_rmsnorm(x2, png_ref[...]), pgw_ref[...]))
    x3 = x2 + gate * _bdot(pe_ref[0], ppj_ref[...])
    o_ref[0] = _rmsnorm(x3, fng_ref[...])

    @pl.when(pl.program_id(1) == pl.num_programs(1) - 1)
    def _():
        tail_ref[0] = tail_scr[...]


def _back(x, hm, y, bonus, g, zg, pe, state, wts, nb, tr, name):
    gg, r, d = x.shape
    dff = wts["f_down"].shape[0]
    pdim = pe.shape[2]
    rt = state.shape[1]
    w = R_WIDTH
    assert r % tr == 0 and tr >= rt
    tok = lambda c: pl.BlockSpec((1, tr, c), lambda a, i: (a, i, 0))
    const = lambda shp: pl.BlockSpec(shp, lambda a, i: tuple(0 for _ in shp), pipeline_mode=pl.Buffered(1))
    tailspec = pl.BlockSpec((1, rt, dff), lambda a, i: (a, 0, 0))
    return pl.pallas_call(
        functools.partial(_back_kernel, nb=nb, rt=rt, dff=dff),
        grid=(gg, r // tr),
        in_specs=[tok(d), tok(M_WIDTH), tok(w), tok(w), tok(w), tok(2 * d), tok(pdim), tailspec,
                  const((1, w)), const((1, w)), const((M_WIDTH, d)), const((w, d)), const((d, d)),
                  const((1, d)), const((d, 2 * dff)), const((SUBLANES, dff)), const((1, dff)), const((dff, d)),
                  const((1, d)), const((d, d)), const((pdim, d)), const((1, d))],
        out_specs=[tok(d), tailspec],
        out_shape=[jax.ShapeDtypeStruct((gg, r, d), F32), jax.ShapeDtypeStruct((gg, rt, dff), F32)],
        scratch_shapes=[pltpu.VMEM((rt, dff), F32)],
        compiler_params=_cparams("parallel", "arbitrary"),
        name=name,
    )(x, hm, y, bonus, g, zg, pe, state, wts["r_ln_g"], wts["r_ln_b"], wts["w_branch_m"], wts["w_branch_r"],
      wts["w_out"], wts["norm2_g"], wts["f_up"], wts["f_conv_w"], wts["f_conv_b"], wts["f_down"],
      wts["ple_norm_g"], wts["ple_gate_w"], wts["ple_proj"], wts["final_norm_g"])


def _pad_rows(w, rows):
    return jnp.pad(w, ((0, rows - w.shape[0]), (0, 0)))


def _layer(x, pe, st, wts, nb):
    g, r, d = x.shape
    n = g * r
    nseq = g * nb
    nsteps = r // nb
    xf = x.reshape(n, d)
    flat = lambda a: a.reshape(n, -1)

    tr = FRONT_ROWS
    (qk, zv, zo, zg, zif, rr, rw, rk, rv, ra, rb, rg, bonus, mtail, rtail) = _front(
        x, st["mconv"], st["rshift"], wts, nb, tr, "front")

    to_seq = (lambda a: a.reshape(nseq, nsteps, a.shape[-1])) if nb == 1 else \
        (lambda a: a.reshape(nsteps, nseq, a.shape[-1]).transpose(1, 0, 2))
    from_seq = (lambda a: a.reshape(n, -1)) if nb == 1 else (lambda a: a.transpose(1, 0, 2).reshape(n, -1))

    hm, mc, mn, mm = _mlstm(to_seq(flat(qk)), to_seq(flat(zv)), to_seq(flat(zo)), to_seq(flat(zif)),
                            wts["m_gate_bias"], wts["m_norm_g"], st["mC"], st["mn"], st["mm"], "mlstm")

    y, s_new = _rscan([to_seq(flat(o)) for o in (rr, rw, rk, rv, ra, rb)], st["rS"], "rscan")

    grp = lambda a: a.reshape(g, r, a.shape[-1])
    tr_b = max(FFN_ROWS // (1 if nb == 1 else 2), st["fconv"].shape[1])
    yout, utail = _back(x, grp(from_seq(hm)), grp(from_seq(y)), bonus, rg, zg, pe, st["fconv"], wts, nb, tr_b, "back")
    return yout, dict(mtail=mtail, rtail=rtail, utail=utail, mC=mc, mn=mn, mm=mm, rS=s_new)


def kernel(x_prompt, x_sample, state_mlstm_conv, state_mlstm_C, state_mlstm_n, state_mlstm_m,
           state_rwkv_shift, state_rwkv_S, state_ffn_conv, p_prompt, p_sample,
           norm1_g, w_in, m_conv_w, m_conv_b, m_i_bias, m_f_bias, m_norm_g, w_branch_m,
           r_mix, r_w0, r_w2, r_a0, r_a2, r_g2, r_kk, r_ka, r_rk, r_ln_g, r_ln_b, w_branch_r,
           w_out, norm2_g, f_up, f_conv_w, f_conv_b, f_down, ple_norm_g, ple_gate_w, ple_proj,
           final_norm_g):
    assert norm1_g.shape[0] == 1, "single trunk layer"
    bp, tp, d = x_prompt.shape
    bs, ts, _ = x_sample.shape
    dff = f_down.shape[1]
    mw2 = 2 * M_WIDTH
    row = lambda a: a.reshape(1, -1).astype(F32)

    wi = w_in[0]
    c_if = 4 * M_WIDTH
    c_r = c_if + 2 * M_HEADS
    c_g = c_r + R_COLS
    lora = jnp.zeros((LANES, R_WIDTH), F32)
    wts = dict(
        norm1_g=row(norm1_g[0]),
        w_m=wi[:, :c_if].astype(BF16), w_r=wi[:, c_r:c_g].astype(BF16), w_g=wi[:, c_g:].astype(BF16),
        w_if=jnp.pad(wi[:, c_if:c_r], ((0, 0), (0, LANES - 2 * M_HEADS))).astype(BF16),
        m_conv_w=_pad_rows(m_conv_w[0], SUBLANES), m_conv_b=row(m_conv_b[0]),
        m_qk_scale=jnp.concatenate([jnp.ones((1, M_WIDTH), F32),
                                    jnp.full((1, M_WIDTH), M_HEAD_DIM ** -0.5, F32)], axis=1),
        m_gate_bias=jnp.pad(jnp.concatenate([m_i_bias[0], m_f_bias[0]]).reshape(1, -1),
                            ((0, 0), (0, LANES - 2 * M_HEADS))),
        m_norm_g=jnp.broadcast_to(m_norm_g[0].reshape(M_HEADS, M_HEAD_DIM, 1), (M_HEADS, M_HEAD_DIM, MLSTM_CHUNK)),
        w_branch_m=w_branch_m[0].astype(BF16),
        r_mix=row(r_mix[0]), r_w0=row(r_w0[0]), r_a0=row(r_a0[0]),
        r_w2=lora.at[:R_DECAY_LORA].set(r_w2[0]).astype(BF16),
        r_a2=lora.at[R_DECAY_LORA:R_DECAY_LORA + R_A_LORA].set(r_a2[0]).astype(BF16),
        r_g2=r_g2[0].astype(BF16), r_kk=row(r_kk[0]), r_ka=row(r_ka[0]), r_rk=row(r_rk[0]),
        r_ln_g=row(r_ln_g[0]), r_ln_b=row(r_ln_b[0]), w_branch_r=w_branch_r[0].astype(BF16),
        w_out=w_out[0].astype(BF16), norm2_g=row(norm2_g[0]), f_up=f_up[0].astype(BF16),
        f_conv_w=_pad_rows(f_conv_w[0], SUBLANES), f_conv_b=row(f_conv_b[0]),
        f_down=f_down[0].astype(BF16), ple_norm_g=row(ple_norm_g[0]), ple_gate_w=ple_gate_w[0].astype(BF16),
        ple_proj=ple_proj[0].astype(BF16), final_norm_g=row(final_norm_g),
    )

    def seq_states(c, nv, m):
        b = c.shape[0]
        nv8 = jnp.pad(nv, ((0, 0), (0, SUBLANES - M_HEADS), (0, 0)))
        m8 = jnp.pad(jnp.broadcast_to(m[:, :, None], (b, M_HEADS, LANES)), ((0, 0), (0, SUBLANES - M_HEADS), (0, 0)))
        return c, nv8, m8

    zc, zn, zm = seq_states(jnp.zeros((bp, M_HEADS, M_HEAD_DIM, M_HEAD_DIM), F32),
                            jnp.zeros((bp, M_HEADS, M_HEAD_DIM), F32), jnp.zeros((bp, M_HEADS), F32))
    st_p = dict(mconv=jnp.zeros((bp, SUBLANES, mw2), F32), rshift=jnp.zeros((bp, SUBLANES, R_COLS), F32),
                fconv=jnp.zeros((bp, SUBLANES, dff), F32), mC=zc, mn=zn, mm=zm,
                rS=jnp.zeros((bp, R_HEADS, R_HEAD_DIM, R_HEAD_DIM), F32))
    y_p, new_p = _layer(x_prompt, p_prompt[0], st_p, wts, 1)

    tmaj = lambda a: a.transpose(1, 0, 2).reshape(1, a.shape[0] * a.shape[1], a.shape[2])
    sc, sn, sm = seq_states(state_mlstm_C[0], state_mlstm_n[0], state_mlstm_m[0])
    st_s = dict(mconv=tmaj(state_mlstm_conv[0]), rshift=state_rwkv_shift[0][None], fconv=tmaj(state_ffn_conv[0]),
                mC=sc, mn=sn, mm=sm, rS=state_rwkv_S[0])
    y_s, new_s = _layer(tmaj(x_sample), tmaj(p_sample[0]), st_s, wts, bs)
    y_sample = y_s.reshape(ts, bs, d).transpose(1, 0, 2)

    def finish(new, b, tmajor):
        def last(tail, k):
            if tmajor:
                return tail.reshape(-1, b, tail.shape[-1])[-k:].transpose(1, 0, 2)
            return tail[:, tail.shape[1] - k:]
        return (last(new["mtail"], M_CONV - 1)[None], new["mC"][None], new["mn"][:, :M_HEADS][None],
                new["mm"][:, :M_HEADS, 0][None], last(new["rtail"], 1)[:, 0][None],
                new["rS"][None], last(new["utail"], F_CONV - 1)[None])

    return (y_p, y_sample) + finish(new_p, bp, False) + finish(new_s, bs, True)
```
